```python
import jax, jax.numpy as jnp
from jax import lax
import numpy as np

D_MODEL = 1024
BATCH = 32
SEQ = 2048
DEPTH = 1

D_RNN = 1024
N_LRU_BLOCKS = 16
LRU_BLOCK = D_RNN // N_LRU_BLOCKS
CONV_WIDTH = 4
LRU_C = 8.0
ATTN_GROUPS = ((128, 1), (512, 4), (2048, 16))
N_GROUPS = len(ATTN_GROUPS)
HEADS_PER_GROUP = 4
HEAD_DIM = 128
ATTN_WIDTH = N_GROUPS * HEADS_PER_GROUP * HEAD_DIM
ATTN_OUT_WIDTH = HEADS_PER_GROUP * HEAD_DIM
ROPE_DIM = HEAD_DIM // 4
ROPE_THETA = 500000.0
Q_BLOCK = 128
N_BRANCHES = 2
D_FF = ((8 * D_MODEL // 3 + 255) // 256) * 256
IN_WIDTH = 2 * D_RNN + 3 * ATTN_WIDTH + N_BRANCHES * D_MODEL
EPS = 1e-6
NEG = -1e30

kernel_name = "hawk_dilated_attn_hybrid_block"


def rms_norm(x, g):
    xf = x.astype(jnp.float32)
    y = xf * lax.rsqrt(jnp.mean(xf * xf, axis=-1, keepdims=True) + EPS)
    return (y * g.astype(jnp.float32)).astype(x.dtype)


def causal_depthwise_conv(x, w, b):
    y = lax.conv_general_dilated(
        x, w[:, None, :].astype(x.dtype), window_strides=(1,),
        padding=((CONV_WIDTH - 1, 0),), dimension_numbers=('NWC', 'WIO', 'NWC'),
        feature_group_count=x.shape[-1])
    return y + b.astype(x.dtype)


def rg_lru(x, w_rg, b_rg, w_ig, b_ig, lam):
    B, S, _ = x.shape
    xf = x.astype(jnp.float32)
    xb = xf.reshape(B, S, N_LRU_BLOCKS, LRU_BLOCK)
    r = jax.nn.sigmoid(jnp.einsum('bsnc,ncd->bsnd', xb, w_rg.astype(jnp.float32)).reshape(B, S, D_RNN) + b_rg.astype(jnp.float32))
    i = jax.nn.sigmoid(jnp.einsum('bsnc,ncd->bsnd', xb, w_ig.astype(jnp.float32)).reshape(B, S, D_RNN) + b_ig.astype(jnp.float32))
    log_a = -LRU_C * r * jax.nn.softplus(-lam.astype(jnp.float32))
    a = jnp.exp(log_a)
    mult = jnp.sqrt(-jnp.expm1(2.0 * log_a))
    mult = jnp.where(jnp.arange(S)[None, :, None] == 0, 1.0, mult)
    u = mult * (i * xf)

    def combine(left, right):
        a1, b1 = left
        a2, b2 = right
        return a1 * a2, a2 * b1 + b2

    _, h = lax.associative_scan(combine, (a, u), axis=1)
    return h.astype(x.dtype)


def partial_rope(t, cos, sin):
    half = ROPE_DIM // 2
    t1 = t[..., :half]
    t2 = t[..., half:ROPE_DIM]
    return jnp.concatenate([t1 * cos - t2 * sin, t2 * cos + t1 * sin, t[..., ROPE_DIM:]], axis=-1)


def dilated_window_attention(q, k, v, window, dilation):
    B, S, H, Dh = q.shape
    w_sub = window // dilation
    L = -(-S // dilation)
    nb = -(-L // Q_BLOCK)
    Lp = nb * Q_BLOCK
    pad = Lp * dilation - S

    def to_blocks(t):
        t = jnp.pad(t.astype(jnp.float32), ((0, 0), (0, pad), (0, 0), (0, 0)))
        t = t.reshape(B, Lp, dilation, H, Dh).transpose(0, 2, 1, 3, 4)
        return t.reshape(B, dilation, nb, Q_BLOCK, H, Dh)

    def with_prev(t):
        prev = jnp.pad(t[:, :, :-1], ((0, 0), (0, 0), (1, 0), (0, 0), (0, 0), (0, 0)))
        return jnp.concatenate([prev, t], axis=3)

    qb = to_blocks(q)
    kb = with_prev(to_blocks(k))
    vb = with_prev(to_blocks(v))

    qi = jnp.arange(Q_BLOCK)[:, None]
    kj = jnp.arange(2 * Q_BLOCK)[None, :]
    diff = qi + Q_BLOCK - kj
    band = (diff >= 0) & (diff <= w_sub)
    has_prev = (jnp.arange(nb)[:, None, None] > 0) | (kj[None] >= Q_BLOCK)
    mask = band[None] & has_prev

    s = jnp.einsum('brnqhc,brnkhc->brnhqk', qb, kb)
    s = jnp.where(mask[None, None, :, None], s, NEG)
    m = jnp.max(s, axis=-1, keepdims=True)
    p = jnp.exp(s - m)
    den = jnp.sum(p, axis=-1, keepdims=True)
    o = jnp.einsum('brnhqk,brnkhc->brnqhc', p, vb) / jnp.swapaxes(den, 3, 4)
    lse = jnp.swapaxes((m + jnp.log(den))[..., 0], 3, 4)

    o = o.reshape(B, dilation, Lp, H, Dh).transpose(0, 2, 1, 3, 4).reshape(B, Lp * dilation, H, Dh)[:, :S]
    lse = lse.reshape(B, dilation, Lp, H).transpose(0, 2, 1, 3).reshape(B, Lp * dilation, H)[:, :S]
    return o, lse


def setup_inputs(seed: int = 0) -> dict:
    key = jax.random.key(seed)
    ks = jax.random.split(key, 24)
    f32 = jnp.float32

    def nrm(k, shape, fan_in):
        return jax.random.normal(k, shape, f32) * (fan_in ** -0.5)

    def gain(k):
        return 1.0 + 0.02 * jax.random.normal(k, (DEPTH, D_MODEL), f32)

    x = jax.random.normal(ks[0], (BATCH, SEQ, D_MODEL), f32)
    offset = jax.random.randint(ks[1], (BATCH, 1), 0, 4096, dtype=jnp.int32)
    positions = offset + jnp.arange(SEQ, dtype=jnp.int32)[None, :]
    u = jax.random.uniform(ks[2], (DEPTH, D_RNN), f32, 0.9, 0.999)
    a_base = u ** (1.0 / LRU_C)
    lru_lambda = jnp.log(a_base) - jnp.log1p(-a_base)
    return {
        "x": x,
        "positions": positions,
        "pre_mix_norm": gain(ks[3]),
        "w_in": nrm(ks[4], (DEPTH, D_MODEL, IN_WIDTH), D_MODEL),
        "conv_w": nrm(ks[5], (DEPTH, CONV_WIDTH, D_RNN), CONV_WIDTH),
        "conv_b": 0.01 * jax.random.normal(ks[6], (DEPTH, D_RNN), f32),
        "w_rg": nrm(ks[7], (DEPTH, N_LRU_BLOCKS, LRU_BLOCK, LRU_BLOCK), LRU_BLOCK),
        "b_rg": 0.01 * jax.random.normal(ks[8], (DEPTH, D_RNN), f32),
        "w_ig": nrm(ks[9], (DEPTH, N_LRU_BLOCKS, LRU_BLOCK, LRU_BLOCK), LRU_BLOCK),
        "b_ig": 0.01 * jax.random.normal(ks[10], (DEPTH, D_RNN), f32),
        "lru_lambda": lru_lambda,
        "w_lru_proj": nrm(ks[11], (DEPTH, D_RNN, D_MODEL), D_RNN),
        "w_attn_proj": nrm(ks[12], (DEPTH, ATTN_OUT_WIDTH, D_MODEL), ATTN_OUT_WIDTH),
        "w_out": nrm(ks[13], (DEPTH, D_MODEL, D_MODEL), D_MODEL),
        "post_mix_norm": gain(ks[14]),
        "pre_ffn_norm": gain(ks[15]),
        "w_ffn_gate": nrm(ks[16], (DEPTH, D_MODEL, D_FF), D_MODEL),
        "w_ffn_up": nrm(ks[17], (DEPTH, D_MODEL, D_FF), D_MODEL),
        "w_ffn_down": nrm(ks[18], (DEPTH, D_FF, D_MODEL), D_FF),
        "post_ffn_norm": gain(ks[19]),
    }


def reference(x, positions, pre_mix_norm, w_in, conv_w, conv_b, w_rg, b_rg, w_ig, b_ig,
              lru_lambda, w_lru_proj, w_attn_proj, w_out, post_mix_norm, pre_ffn_norm,
              w_ffn_gate, w_ffn_up, w_ffn_down, post_ffn_norm):
    B, S, _ = x.shape
    dt = x.dtype
    inv_freq = ROPE_THETA ** (-jnp.arange(0, ROPE_DIM, 2, dtype=jnp.float32) / ROPE_DIM)
    ang = positions.astype(jnp.float32)[..., None] * inv_freq
    cos = jnp.cos(ang)[:, :, None, :].astype(dt)
    sin = jnp.sin(ang)[:, :, None, :].astype(dt)
    split_at = np.cumsum([D_RNN, D_RNN, ATTN_WIDTH, ATTN_WIDTH, ATTN_WIDTH]).tolist()

    for l in range(DEPTH):
        h = rms_norm(x, pre_mix_norm[l])
        proj = h @ w_in[l].astype(dt)
        xr, gr, q, k, v, gates = jnp.split(proj, split_at, axis=-1)

        xr = causal_depthwise_conv(xr, conv_w[l], conv_b[l])
        xr = rg_lru(xr, w_rg[l], b_rg[l], w_ig[l], b_ig[l], lru_lambda[l])
        y_lru = (xr * jax.nn.gelu(gr)) @ w_lru_proj[l].astype(dt)

        q = q.reshape(B, S, N_GROUPS, HEADS_PER_GROUP, HEAD_DIM)
        k = k.reshape(B, S, N_GROUPS, HEADS_PER_GROUP, HEAD_DIM)
        v = v.reshape(B, S, N_GROUPS, HEADS_PER_GROUP, HEAD_DIM)
        q = partial_rope(q, cos[:, :, None], sin[:, :, None]) * (HEAD_DIM ** -0.5)
        k = partial_rope(k, cos[:, :, None], sin[:, :, None])
        outs, lses = [], []
        for g, (window, dilation) in enumerate(ATTN_GROUPS):
            o_g, lse_g = dilated_window_attention(q[:, :, g], k[:, :, g], v[:, :, g], window, dilation)
            outs.append(o_g)
            lses.append(lse_g)
        o = jnp.stack(outs, axis=2)
        wts = jax.nn.softmax(jnp.stack(lses, axis=2), axis=2)
        o = jnp.sum(wts[..., None] * o, axis=2).reshape(B, S, ATTN_OUT_WIDTH).astype(dt)
        y_attn = o @ w_attn_proj[l].astype(dt)

        g_lru, g_attn = jnp.split(jax.nn.sigmoid(gates), N_BRANCHES, axis=-1)
        mix = (g_lru * y_lru + g_attn * y_attn) @ w_out[l].astype(dt)
        x = x + rms_norm(mix, post_mix_norm[l])

        h = rms_norm(x, pre_ffn_norm[l])
        f = (jax.nn.silu(h @ w_ffn_gate[l].astype(dt)) * (h @ w_ffn_up[l].astype(dt))) @ w_ffn_down[l].astype(dt)
        x = x + rms_norm(f, post_ffn_norm[l])
    return x
```

```python
import functools
import math

import jax
import jax.numpy as jnp
from jax import lax
from jax.experimental import pallas as pl
from jax.experimental.pallas import tpu as pltpu

D_MODEL = 1024
D_RNN = 1024
N_LRU_BLOCKS = 16
LRU_BLOCK = D_RNN // N_LRU_BLOCKS
CONV_WIDTH = 4
LRU_C = 8.0
ATTN_GROUPS = ((128, 1), (512, 4), (2048, 16))
N_GROUPS = len(ATTN_GROUPS)
HEADS_PER_GROUP = 4
HEAD_DIM = 128
GROUP_WIDTH = HEADS_PER_GROUP * HEAD_DIM
ATTN_WIDTH = N_GROUPS * GROUP_WIDTH
ROPE_DIM = HEAD_DIM // 4
ROPE_THETA = 500000.0
Q_BLOCK = 128
D_FF = ((8 * D_MODEL // 3 + 255) // 256) * 256
IN_WIDTH = 2 * D_RNN + 3 * ATTN_WIDTH + 2 * D_MODEL
EPS = 1e-6
NEG = -1e30

V7X_LANES = 128
V7X_SUBLANES = 8
V7X_MXU_DIM = 256
V7X_VMEM_BYTES = 64 * 1024 * 1024
VMEM_LIMIT_BYTES = V7X_VMEM_BYTES - 8 * 1024 * 1024

BF16 = jnp.bfloat16
F32 = jnp.float32

LRU_PACK = V7X_MXU_DIM // LRU_BLOCK
N_LRU_TILES = N_LRU_BLOCKS // LRU_PACK

TM = 256
TK = 256


def _rms_norm(x, g):
    return x * lax.rsqrt(jnp.mean(x * x, axis=-1, keepdims=True) + EPS) * g


def _sigmoid(x):
    return 1.0 / (1.0 + jnp.exp(-x))


def _gelu_tanh(x):
    c = math.sqrt(2.0 / math.pi)
    return 0.5 * x * (1.0 + jnp.tanh(c * (x + 0.044715 * (x * x * x))))


def _const_spec(shape):
    nd = len(shape)
    return pl.BlockSpec(shape, lambda *_: (0,) * nd, pipeline_mode=pl.Buffered(1))


IN_CHUNK = GROUP_WIDTH


def _in_proj_kernel(x_ref, pos_ref, g_ref, freq_ref, w_ref,
                    xr_ref, gr_ref, q1_ref, q2_ref, q3_ref, k1_ref, k2_ref, k3_ref,
                    v1_ref, v2_ref, v3_ref, gates_ref):
    x = x_ref[...]
    h = _rms_norm(x, g_ref[...]).astype(BF16)

    ang = pos_ref[...].astype(F32) * freq_ref[...]
    cos = jnp.cos(ang)
    sin = jnp.sin(ang)
    lane = lax.broadcasted_iota(jnp.int32, ang.shape, 1)
    half = ROPE_DIM // 2
    sin_lo = jnp.where(lane < half, -sin, 0.0)
    sin_hi = jnp.where(lane >= half, sin, 0.0)
    scale = HEAD_DIM ** -0.5

    def rope(y, c, s_lo, s_hi):
        outs = []
        for hd in range(HEADS_PER_GROUP):
            t = y[:, hd * HEAD_DIM:(hd + 1) * HEAD_DIM]
            up = pltpu.roll(t, HEAD_DIM - half, axis=1)
            dn = pltpu.roll(t, half, axis=1)
            outs.append(t * c + up * s_lo + dn * s_hi)
        return jnp.concatenate(outs, axis=1)

    def chunk(ci):
        return jnp.dot(h, w_ref[:, ci * IN_CHUNK:(ci + 1) * IN_CHUNK],
                       preferred_element_type=F32)

    ci = 0
    for ref in (xr_ref, gr_ref):
        for j in range(D_RNN // IN_CHUNK):
            ref[:, j * IN_CHUNK:(j + 1) * IN_CHUNK] = chunk(ci).astype(ref.dtype)
            ci += 1
    for ref in (q1_ref, q2_ref, q3_ref):
        ref[...] = rope(chunk(ci), cos * scale, sin_lo * scale, sin_hi * scale).astype(ref.dtype)
        ci += 1
    for ref in (k1_ref, k2_ref, k3_ref):
        ref[...] = rope(chunk(ci), cos, sin_lo, sin_hi).astype(ref.dtype)
        ci += 1
    for ref in (v1_ref, v2_ref, v3_ref):
        ref[...] = chunk(ci).astype(ref.dtype)
        ci += 1
    for j in range(2 * D_MODEL // IN_CHUNK):
        gates_ref[:, j * IN_CHUNK:(j + 1) * IN_CHUNK] = chunk(ci).astype(gates_ref.dtype)
        ci += 1


def _in_proj(x2, pos2, g, freq, w_in, tm):
    T = x2.shape[0]
    row = lambda w: pl.BlockSpec((tm, w), lambda i: (i, 0))
    out_widths = [D_RNN, D_RNN] + [GROUP_WIDTH] * 9 + [2 * D_MODEL]
    return pl.pallas_call(
        _in_proj_kernel,
        grid=(T // tm,),
        in_specs=[row(D_MODEL), row(1), _const_spec((1, D_MODEL)), _const_spec((1, HEAD_DIM)),
                  _const_spec((D_MODEL, IN_WIDTH))],
        out_specs=[row(w) for w in out_widths],
        out_shape=[jax.ShapeDtypeStruct((T, w), BF16) for w in out_widths],
        compiler_params=pltpu.CompilerParams(
            dimension_semantics=("arbitrary",), vmem_limit_bytes=VMEM_LIMIT_BYTES),
        name="in_proj",
    )(x2, pos2, g, freq, w_in)


def _lru_kernel(xr_ref, gr_ref, cw_ref, cb_ref, wrg_ref, brg_ref, wig_ref, big_ref, lam_ref,
                out_ref, tail_ref, carry_ref, a_ref, u_ref, h_ref, *, tk):
    t_idx = pl.program_id(1)

    @pl.when(t_idx == 0)
    def _():
        tail_ref[...] = jnp.zeros_like(tail_ref)
        carry_ref[...] = jnp.zeros_like(carry_ref)

    x = xr_ref[...].astype(F32)
    tail = tail_ref[...]
    tail_ref[...] = x[tk - V7X_SUBLANES:, :]
    row8 = lax.broadcasted_iota(jnp.int32, (V7X_SUBLANES, D_RNN), 0)

    y = x * cw_ref[CONV_WIDTH - 1:CONV_WIDTH, :] + cb_ref[...]
    for s in range(1, CONV_WIDTH):
        xs = pltpu.roll(x, s, axis=0)
        head = jnp.where(row8 < s, pltpu.roll(tail, s, axis=0), xs[:V7X_SUBLANES, :])
        xs = jnp.concatenate([head, xs[V7X_SUBLANES:, :]], axis=0)
        y = y + xs * cw_ref[CONV_WIDTH - 1 - s:CONV_WIDTH - s, :]

    yb = y.astype(BF16)
    r_parts, i_parts = [], []
    for j in range(N_LRU_TILES):
        sl = slice(j * V7X_MXU_DIM, (j + 1) * V7X_MXU_DIM)
        r_parts.append(jnp.dot(yb[:, sl], wrg_ref[j], preferred_element_type=F32))
        i_parts.append(jnp.dot(yb[:, sl], wig_ref[j], preferred_element_type=F32))
    r = _sigmoid(jnp.concatenate(r_parts, axis=1) + brg_ref[...])
    ig = _sigmoid(jnp.concatenate(i_parts, axis=1) + big_ref[...])

    nl = -lam_ref[...]
    softplus = jnp.maximum(nl, 0.0) + jnp.log1p(jnp.exp(-jnp.abs(nl)))
    log_a = (-LRU_C) * r * softplus
    a = jnp.exp(log_a)
    mult = jnp.sqrt(1.0 - a * a)
    pos = t_idx * tk + lax.broadcasted_iota(jnp.int32, (tk, 1), 0)
    mult = jnp.where(pos == 0, 1.0, mult)
    a_ref[...] = a
    u_ref[...] = mult * (ig * y)

    def group(j, carry):
        rows = pl.ds(pl.multiple_of(j * V7X_SUBLANES, V7X_SUBLANES), V7X_SUBLANES)
        a8 = a_ref[rows, :]
        u8 = u_ref[rows, :]
        for s in (1, 2, 4):
            keep = row8 >= s
            a_m = jnp.where(keep, a8, 0.0)
            u8 = u8 + a_m * pltpu.roll(u8, s, axis=0)
            a8 = jnp.where(keep, a8 * pltpu.roll(a8, s, axis=0), a8)
        h8 = u8 + a8 * carry
        h_ref[rows, :] = h8
        return jnp.broadcast_to(h8[V7X_SUBLANES - 1:V7X_SUBLANES, :], (V7X_SUBLANES, D_RNN))

    carry_ref[...] = lax.fori_loop(0, tk // V7X_SUBLANES, group, carry_ref[...], unroll=4)

    out_ref[...] = (h_ref[...] * _gelu_tanh(gr_ref[...].astype(F32))).astype(out_ref.dtype)


def _lru_mixer(xr, gr, conv_w, conv_b, wrg, brg, wig, big, lam, tk):
    B, S, _ = xr.shape
    tile = pl.BlockSpec((None, tk, D_RNN), lambda b, t: (b, t, 0))
    vec = _const_spec((1, D_RNN))
    gate_w = _const_spec((N_LRU_TILES, V7X_MXU_DIM, V7X_MXU_DIM))
    return pl.pallas_call(
        functools.partial(_lru_kernel, tk=tk),
        grid=(B, S // tk),
        in_specs=[tile, tile, _const_spec((CONV_WIDTH, D_RNN)), vec, gate_w, vec, gate_w, vec, vec],
        out_specs=tile,
        out_shape=jax.ShapeDtypeStruct((B, S, D_RNN), BF16),
        scratch_shapes=[pltpu.VMEM((V7X_SUBLANES, D_RNN), F32),
                        pltpu.VMEM((V7X_SUBLANES, D_RNN), F32),
                        pltpu.VMEM((tk, D_RNN), F32),
                        pltpu.VMEM((tk, D_RNN), F32),
                        pltpu.VMEM((tk, D_RNN), F32)],
        compiler_params=pltpu.CompilerParams(
            dimension_semantics=("arbitrary", "arbitrary"), vmem_limit_bytes=VMEM_LIMIT_BYTES),
        name="lru_mixer",
    )(xr, gr, conv_w, conv_b, wrg, brg, wig, big, lam)


def _attn_kernel(q_ref, k_ref, v_ref, o_ref, lse_ref, *, n_res, n_blk):
    qi = lax.broadcasted_iota(jnp.int32, (Q_BLOCK, 2 * Q_BLOCK), 0)
    kj = lax.broadcasted_iota(jnp.int32, (Q_BLOCK, 2 * Q_BLOCK), 1)
    band2 = (kj >= qi) & (kj <= qi + Q_BLOCK)
    band1 = (lax.broadcasted_iota(jnp.int32, (Q_BLOCK, Q_BLOCK), 1)
             <= lax.broadcasted_iota(jnp.int32, (Q_BLOCK, Q_BLOCK), 0))
    lane = lax.broadcasted_iota(jnp.int32, (Q_BLOCK, HEAD_DIM), 1)
    seg = HEAD_DIM // HEADS_PER_GROUP

    def block(r, q_rows, kv_rows, band):
        lse_tile = jnp.zeros((Q_BLOCK, HEAD_DIM), F32)
        for hd in range(HEADS_PER_GROUP):
            cols = slice(r * GROUP_WIDTH + hd * HEAD_DIM, r * GROUP_WIDTH + (hd + 1) * HEAD_DIM)
            q = q_ref[q_rows, cols]
            k = k_ref[kv_rows, cols]
            v = v_ref[kv_rows, cols]
            s = lax.dot_general(q, k, (((1,), (1,)), ((), ())), preferred_element_type=F32)
            s = jnp.where(band, s, NEG)
            m = jnp.max(s, axis=-1, keepdims=True)
            p = jnp.exp(s - m)
            den = jnp.sum(p, axis=-1, keepdims=True)
            o = jnp.dot(p.astype(BF16), v, preferred_element_type=F32) / den
            o_ref[q_rows, cols] = o.astype(o_ref.dtype)
            lse_tile = jnp.where(lane // seg == hd, m + jnp.log(den), lse_tile)
        lse_ref[q_rows, r * HEAD_DIM:(r + 1) * HEAD_DIM] = lse_tile

    for r in range(n_res):
        first = pl.ds(0, Q_BLOCK)
        block(r, first, first, band1)
        if n_blk > 1:
            def body(n, _, r=r):
                q_rows = pl.ds(pl.multiple_of(n * Q_BLOCK, Q_BLOCK), Q_BLOCK)
                kv_rows = pl.ds(pl.multiple_of((n - 1) * Q_BLOCK, Q_BLOCK), 2 * Q_BLOCK)
                block(r, q_rows, kv_rows, band2)
                return 0
            lax.fori_loop(1, n_blk, body, 0)


def _attention_group(q, k, v, dilation, res_per_step):
    B, S, _ = q.shape
    L = S // dilation
    n_blk = L // Q_BLOCK
    view = lambda t: t.reshape(B, L, dilation * GROUP_WIDTH)
    n_steps = dilation // res_per_step
    qkv_spec = pl.BlockSpec((None, L, res_per_step * GROUP_WIDTH), lambda b, j: (b, 0, j))
    lse_spec = pl.BlockSpec((None, L, res_per_step * HEAD_DIM), lambda b, j: (b, 0, j))
    o, lse = pl.pallas_call(
        functools.partial(_attn_kernel, n_res=res_per_step, n_blk=n_blk),
        grid=(B, n_steps),
        in_specs=[qkv_spec, qkv_spec, qkv_spec],
        out_specs=[qkv_spec, lse_spec],
        out_shape=[jax.ShapeDtypeStruct((B, L, dilation * GROUP_WIDTH), BF16),
                   jax.ShapeDtypeStruct((B, L, dilation * HEAD_DIM), F32)],
        compiler_params=pltpu.CompilerParams(
            dimension_semantics=("arbitrary", "arbitrary"), vmem_limit_bytes=VMEM_LIMIT_BYTES),
        name=f"attention_d{dilation}",
    )(view(q), view(k), view(v))
    return o.reshape(B * S, GROUP_WIDTH), lse.reshape(B * S, HEAD_DIM)


FF_CHUNK = 2 * V7X_MXU_DIM


def _merge_ffn_kernel(x_ref, gated_ref, o1_ref, o2_ref, o3_ref, l1_ref, l2_ref, l3_ref, gates_ref,
                      wl_ref, wa_ref, wo_ref, gpost_ref, gpre_ref, wg_ref, wu_ref, wd_ref, gffn_ref,
                      out_ref, act_ref):
    y_lru = jnp.dot(gated_ref[...], wl_ref[...], preferred_element_type=F32)

    l1, l2, l3 = l1_ref[...], l2_ref[...], l3_ref[...]
    mx = jnp.maximum(jnp.maximum(l1, l2), l3)
    e1, e2, e3 = jnp.exp(l1 - mx), jnp.exp(l2 - mx), jnp.exp(l3 - mx)
    inv = 1.0 / (e1 + e2 + e3)
    seg = HEAD_DIM // HEADS_PER_GROUP
    heads = []
    for hd in range(HEADS_PER_GROUP):
        cols = slice(hd * HEAD_DIM, (hd + 1) * HEAD_DIM)
        acc = None
        for e, o_ref in ((e1, o1_ref), (e2, o2_ref), (e3, o3_ref)):
            w = (e * inv)[:, hd * seg:hd * seg + 1]
            term = w * o_ref[:, cols].astype(F32)
            acc = term if acc is None else acc + term
        heads.append(acc)
    o = jnp.concatenate(heads, axis=1).astype(BF16)
    y_attn = jnp.dot(o, wa_ref[...], preferred_element_type=F32)

    g_lru = _sigmoid(gates_ref[:, :D_MODEL].astype(F32))
    g_attn = _sigmoid(gates_ref[:, D_MODEL:].astype(F32))
    merged = (g_lru * y_lru + g_attn * y_attn).astype(BF16)
    mix = jnp.dot(merged, wo_ref[...], preferred_element_type=F32)
    x1 = x_ref[...] + _rms_norm(mix, gpost_ref[...])

    h = _rms_norm(x1, gpre_ref[...]).astype(BF16)
    for c in range(0, D_FF, FF_CHUNK):
        w = min(FF_CHUNK, D_FF - c)
        gate = jnp.dot(h, wg_ref[:, c:c + w], preferred_element_type=F32)
        up = jnp.dot(h, wu_ref[:, c:c + w], preferred_element_type=F32)
        act_ref[:, c:c + w] = (gate * _sigmoid(gate) * up).astype(BF16)
    f = jnp.dot(act_ref[...], wd_ref[...], preferred_element_type=F32)
    out_ref[...] = x1 + _rms_norm(f, gffn_ref[...])


def _merge_ffn(x2, gated, o1, o2, o3, l1, l2, l3, gates, wl, wa, wo, gpost, gpre, wg, wu, wd, gffn, tm):
    T = x2.shape[0]
    row = lambda w: pl.BlockSpec((tm, w), lambda i: (i, 0))
    vec = _const_spec((1, D_MODEL))
    return pl.pallas_call(
        _merge_ffn_kernel,
        grid=(T // tm,),
        in_specs=[row(D_MODEL), row(D_RNN), row(GROUP_WIDTH), row(GROUP_WIDTH), row(GROUP_WIDTH),
                  row(HEAD_DIM), row(HEAD_DIM), row(HEAD_DIM), row(2 * D_MODEL),
                  _const_spec((D_RNN, D_MODEL)), _const_spec((GROUP_WIDTH, D_MODEL)),
                  _const_spec((D_MODEL, D_MODEL)), vec, vec,
                  _const_spec((D_MODEL, D_FF)), _const_spec((D_MODEL, D_FF)),
                  _const_spec((D_FF, D_MODEL)), vec],
        out_specs=row(D_MODEL),
        out_shape=jax.ShapeDtypeStruct((T, D_MODEL), F32),
        scratch_shapes=[pltpu.VMEM((tm, D_FF), BF16)],
        compiler_params=pltpu.CompilerParams(
            dimension_semantics=("arbitrary",), vmem_limit_bytes=VMEM_LIMIT_BYTES),
        name="merge_ffn",
    )(x2, gated, o1, o2, o3, l1, l2, l3, gates, wl, wa, wo, gpost, gpre, wg, wu, wd, gffn)


def _pack_lru_gate(w):
    w4 = w.reshape(N_LRU_TILES, LRU_PACK, LRU_BLOCK, LRU_BLOCK)
    eye = jnp.eye(LRU_PACK, dtype=w.dtype)
    packed = jnp.einsum('jacd,ab->jacbd', w4, eye)
    return packed.reshape(N_LRU_TILES, V7X_MXU_DIM, V7X_MXU_DIM).astype(BF16)


def kernel(x, positions, pre_mix_norm, w_in, conv_w, conv_b, w_rg, b_rg, w_ig, b_ig, lru_lambda,
           w_lru_proj, w_attn_proj, w_out, post_mix_norm, pre_ffn_norm, w_ffn_gate, w_ffn_up,
           w_ffn_down, post_ffn_norm):
    B, S, D = x.shape
    assert D == D_MODEL and S % (ATTN_GROUPS[-1][1] * Q_BLOCK) == 0
    assert pre_mix_norm.shape[0] == 1, "single-layer block"
    T = B * S
    tm, tk = TM, TK

    inv_freq = ROPE_THETA ** (-jnp.arange(0, ROPE_DIM, 2, dtype=F32) / ROPE_DIM)
    freq = jnp.concatenate([inv_freq, inv_freq, jnp.zeros((HEAD_DIM - ROPE_DIM,), F32)])[None, :]

    x2 = x.reshape(T, D)
    pos2 = positions.reshape(T, 1)
    row = lambda p: p[0][None, :]

    xr, gr, q1, q2, q3, k1, k2, k3, v1, v2, v3, gates = _in_proj(
        x2, pos2, row(pre_mix_norm), freq, w_in[0].astype(BF16), tm)

    gated = _lru_mixer(xr.reshape(B, S, D_RNN), gr.reshape(B, S, D_RNN), conv_w[0], row(conv_b),
                       _pack_lru_gate(w_rg[0]), row(b_rg), _pack_lru_gate(w_ig[0]), row(b_ig),
                       row(lru_lambda), tk)

    outs = []
    for (q, k, v), (_, dilation) in zip(((q1, k1, v1), (q2, k2, v2), (q3, k3, v3)), ATTN_GROUPS):
        shp = (B, S, GROUP_WIDTH)
        outs.append(_attention_group(q.reshape(shp), k.reshape(shp), v.reshape(shp), dilation,
                                     res_per_step=min(dilation, 4)))
    (o1, l1), (o2, l2), (o3, l3) = outs

    out = _merge_ffn(x2, gated.reshape(T, D_RNN), o1, o2, o3, l1, l2, l3, gates,
                     w_lru_proj[0].astype(BF16), w_attn_proj[0].astype(BF16), w_out[0].astype(BF16),
                     row(post_mix_norm), row(pre_ffn_norm), w_ffn_gate[0].astype(BF16),
                     w_ffn_up[0].astype(BF16), w_ffn_down[0].astype(BF16), row(post_ffn_norm), tm)
    return out.reshape(B, S, D)
```

```python
import functools
import math

import jax
import jax.numpy as jnp
from jax import lax
from jax.experimental import pallas as pl
from jax.experimental.pallas import tpu as pltpu

D_MODEL = 1024
D_RNN = 1024
N_LRU_BLOCKS = 16
LRU_BLOCK = D_RNN // N_LRU_BLOCKS
CONV_WIDTH = 4
LRU_C = 8.0
ATTN_GROUPS = ((128, 1), (512, 4), (2048, 16))
DILATIONS = tuple(d for _, d in ATTN_GROUPS)
N_GROUPS = len(ATTN_GROUPS)
HEADS_PER_GROUP = 4
HEAD_DIM = 128
GROUP_WIDTH = HEADS_PER_GROUP * HEAD_DIM
ATTN_WIDTH = N_GROUPS * GROUP_WIDTH
ROPE_DIM = HEAD_DIM // 4
ROPE_THETA = 500000.0
Q_BLOCK = 128
D_FF = ((8 * D_MODEL // 3 + 255) // 256) * 256
IN_WIDTH = 2 * D_RNN + 3 * ATTN_WIDTH + 2 * D_MODEL
EPS = 1e-6
NEG = -1e30

V7X_LANES = 128
V7X_SUBLANES = 8
V7X_MXU_DIM = 256
V7X_VMEM_BYTES = 64 * 1024 * 1024
VMEM_LIMIT_BYTES = V7X_VMEM_BYTES - 8 * 1024 * 1024

BF16 = jnp.bfloat16
F32 = jnp.float32

LRU_PACK = V7X_MXU_DIM // LRU_BLOCK
N_LRU_TILES = N_LRU_BLOCKS // LRU_PACK

TM = 256
TK = 256

assert all(w // d == Q_BLOCK for w, d in ATTN_GROUPS), "band logic assumes window == dilation * Q_BLOCK"


def _rms_norm(x, g):
    return x * lax.rsqrt(jnp.mean(x * x, axis=-1, keepdims=True) + EPS) * g


def _sigmoid(x):
    return 0.5 * jnp.tanh(0.5 * x) + 0.5


def _gelu_tanh(x):
    c = math.sqrt(2.0 / math.pi)
    return 0.5 * x * (1.0 + jnp.tanh(c * (x + 0.044715 * (x * x * x))))


def _const_spec(shape):
    nd = len(shape)
    return pl.BlockSpec(shape, lambda *_: (0,) * nd, pipeline_mode=pl.Buffered(1))


def _slabs(width):
    return [slice(c * V7X_LANES, (c + 1) * V7X_LANES) for c in range(width // V7X_LANES)]


def _residue_major(slab_ref, d, n_rows):
    if d == 1:
        return jnp.concatenate([slab_ref[c] for c in range(slab_ref.shape[0])], axis=1)
    n = n_rows // d
    return jnp.concatenate(
        [jnp.concatenate([slab_ref[c, pl.ds(r, n, stride=d), :] for c in range(slab_ref.shape[0])], axis=1)
         for r in range(d)], axis=0)


IN_CHUNK = GROUP_WIDTH


def _in_proj_kernel(x_ref, pos_ref, g_ref, freq_ref, w_ref,
                    xr_ref, gr_ref, q1_ref, q2_ref, q3_ref, k1_ref, k2_ref, k3_ref,
                    v1_ref, v2_ref, v3_ref, gates_ref, h_scr, rope_scr, *, tm):
    h = _rms_norm(x_ref[...], g_ref[...])
    for c, sl in enumerate(_slabs(D_MODEL)):
        h_scr[c] = h[:, sl]

    ang = pos_ref[...].astype(F32) * freq_ref[...]
    cos = jnp.cos(ang)
    sin = jnp.sin(ang)
    lane = lax.broadcasted_iota(jnp.int32, ang.shape, 1)
    half = ROPE_DIM // 2
    rope_scr[0] = cos
    rope_scr[1] = jnp.where(lane < half, -sin, 0.0)
    rope_scr[2] = jnp.where(lane >= half, sin, 0.0)
    scale = HEAD_DIM ** -0.5

    def rope(y, tabs, mul):
        c, s_lo, s_hi = (tabs[:, sl] if mul == 1.0 else tabs[:, sl] * mul for sl in _slabs(3 * HEAD_DIM))
        outs = []
        for sl in _slabs(GROUP_WIDTH):
            t = y[:, sl]
            up = pltpu.roll(t, HEAD_DIM - half, axis=1)
            dn = pltpu.roll(t, half, axis=1)
            outs.append(t * c + up * s_lo + dn * s_hi)
        return jnp.concatenate(outs, axis=1)

    def chunk(lhs, ci):
        return jnp.dot(lhs, w_ref[:, ci * IN_CHUNK:(ci + 1) * IN_CHUNK], preferred_element_type=F32)

    def emit(ref, y, d):
        n = tm // d
        for r in range(d):
            ref[r] = y[r * n:(r + 1) * n, :].astype(ref.dtype)

    q_base = 2 * D_RNN // IN_CHUNK
    gates_base = q_base + 3 * N_GROUPS
    for g, d in enumerate(DILATIONS):
        hp = _residue_major(h_scr, d, tm).astype(BF16)
        tabs = _residue_major(rope_scr, d, tm)
        emit((q1_ref, q2_ref, q3_ref)[g], rope(chunk(hp, q_base + g), tabs, scale), d)
        emit((k1_ref, k2_ref, k3_ref)[g], rope(chunk(hp, q_base + N_GROUPS + g), tabs, 1.0), d)
        emit((v1_ref, v2_ref, v3_ref)[g], chunk(hp, q_base + 2 * N_GROUPS + g), d)
        if d == 1:
            for j in range(D_RNN // IN_CHUNK):
                cols = slice(j * IN_CHUNK, (j + 1) * IN_CHUNK)
                xr_ref[:, cols] = chunk(hp, j).astype(xr_ref.dtype)
                gr_ref[:, cols] = chunk(hp, D_RNN // IN_CHUNK + j).astype(gr_ref.dtype)
            for j in range(2 * D_MODEL // IN_CHUNK):
                cols = slice(j * IN_CHUNK, (j + 1) * IN_CHUNK)
                gates_ref[:, cols] = chunk(hp, gates_base + j).astype(gates_ref.dtype)


def _grouped_spec(d, rows, width, steps_per_seq):
    return pl.BlockSpec((None, d, rows // d, width),
                        lambda i: (i // steps_per_seq, 0, i % steps_per_seq, 0))


def _in_proj(x2, pos2, g, freq, w_in, B, S, tm):
    T = B * S
    nt = S // tm
    row = lambda w: pl.BlockSpec((tm, w), lambda i: (i, 0))
    qkv_specs = [_grouped_spec(d, tm, GROUP_WIDTH, nt) for d in DILATIONS] * 3
    qkv_shapes = [jax.ShapeDtypeStruct((B, d, S // d, GROUP_WIDTH), BF16) for d in DILATIONS] * 3
    flat = lambda w: jax.ShapeDtypeStruct((T, w), BF16)
    return pl.pallas_call(
        functools.partial(_in_proj_kernel, tm=tm),
        grid=(T // tm,),
        in_specs=[row(D_MODEL), row(1), _const_spec((1, D_MODEL)), _const_spec((1, HEAD_DIM)),
                  _const_spec((D_MODEL, IN_WIDTH))],
        out_specs=[row(D_RNN), row(D_RNN)] + qkv_specs + [row(2 * D_MODEL)],
        out_shape=[flat(D_RNN), flat(D_RNN)] + qkv_shapes + [flat(2 * D_MODEL)],
        scratch_shapes=[pltpu.VMEM((D_MODEL // V7X_LANES, tm, V7X_LANES), F32),
                        pltpu.VMEM((3, tm, V7X_LANES), F32)],
        compiler_params=pltpu.CompilerParams(
            dimension_semantics=("arbitrary",), vmem_limit_bytes=VMEM_LIMIT_BYTES),
        name="in_proj",
    )(x2, pos2, g, freq, w_in)


def _lru_kernel(xr_ref, gr_ref, cw_ref, cb_ref, wrg_ref, brg_ref, wig_ref, big_ref, lam_ref,
                out_ref, tail_ref, carry_ref, a_ref, u_ref, h_ref, *, tk):
    t_idx = pl.program_id(1)

    @pl.when(t_idx == 0)
    def _():
        tail_ref[...] = jnp.zeros_like(tail_ref)
        carry_ref[...] = jnp.zeros_like(carry_ref)

    x = xr_ref[...].astype(F32)
    tail = tail_ref[...]
    tail_ref[...] = x[tk - V7X_SUBLANES:, :]
    row8 = lax.broadcasted_iota(jnp.int32, (V7X_SUBLANES, D_RNN), 0)

    y = x * cw_ref[CONV_WIDTH - 1:CONV_WIDTH, :] + cb_ref[...]
    for s in range(1, CONV_WIDTH):
        xs = pltpu.roll(x, s, axis=0)
        head = jnp.where(row8 < s, pltpu.roll(tail, s, axis=0), xs[:V7X_SUBLANES, :])
        xs = jnp.concatenate([head, xs[V7X_SUBLANES:, :]], axis=0)
        y = y + xs * cw_ref[CONV_WIDTH - 1 - s:CONV_WIDTH - s, :]

    yb = y.astype(BF16)
    r_parts, i_parts = [], []
    for j in range(N_LRU_TILES):
        sl = slice(j * V7X_MXU_DIM, (j + 1) * V7X_MXU_DIM)
        r_parts.append(jnp.dot(yb[:, sl], wrg_ref[j], preferred_element_type=F32))
        i_parts.append(jnp.dot(yb[:, sl], wig_ref[j], preferred_element_type=F32))
    r = _sigmoid(jnp.concatenate(r_parts, axis=1) + brg_ref[...])
    ig = _sigmoid(jnp.concatenate(i_parts, axis=1) + big_ref[...])

    nl = -lam_ref[...]
    softplus = jnp.maximum(nl, 0.0) + jnp.log1p(jnp.exp(-jnp.abs(nl)))
    log_a = (-LRU_C) * r * softplus
    a = jnp.exp(log_a)
    mult = jnp.sqrt(1.0 - a * a)
    pos = t_idx * tk + lax.broadcasted_iota(jnp.int32, (tk, 1), 0)
    mult = jnp.where(pos == 0, 1.0, mult)
    a_ref[...] = a
    u_ref[...] = mult * (ig * y)

    def group(j, carry):
        rows = pl.ds(pl.multiple_of(j * V7X_SUBLANES, V7X_SUBLANES), V7X_SUBLANES)
        a8 = a_ref[rows, :]
        u8 = u_ref[rows, :]
        for s in (1, 2, 4):
            keep = row8 >= s
            a_m = jnp.where(keep, a8, 0.0)
            u8 = u8 + a_m * pltpu.roll(u8, s, axis=0)
            a8 = jnp.where(keep, a8 * pltpu.roll(a8, s, axis=0), a8)
        h8 = u8 + a8 * carry
        h_ref[rows, :] = h8
        return jnp.broadcast_to(h8[V7X_SUBLANES - 1:V7X_SUBLANES, :], (V7X_SUBLANES, D_RNN))

    carry_ref[...] = lax.fori_loop(0, tk // V7X_SUBLANES, group, carry_ref[...], unroll=4)

    out_ref[...] = (h_ref[...] * _gelu_tanh(gr_ref[...].astype(F32))).astype(out_ref.dtype)


def _lru_mixer(xr, gr, conv_w, conv_b, wrg, brg, wig, big, lam, tk):
    B, S, _ = xr.shape
    tile = pl.BlockSpec((None, tk, D_RNN), lambda b, t: (b, t, 0))
    vec = _const_spec((1, D_RNN))
    gate_w = _const_spec((N_LRU_TILES, V7X_MXU_DIM, V7X_MXU_DIM))
    return pl.pallas_call(
        functools.partial(_lru_kernel, tk=tk),
        grid=(B, S // tk),
        in_specs=[tile, tile, _const_spec((CONV_WIDTH, D_RNN)), vec, gate_w, vec, gate_w, vec, vec],
        out_specs=tile,
        out_shape=jax.ShapeDtypeStruct((B, S, D_RNN), BF16),
        scratch_shapes=[pltpu.VMEM((V7X_SUBLANES, D_RNN), F32),
                        pltpu.VMEM((V7X_SUBLANES, D_RNN), F32),
                        pltpu.VMEM((tk, D_RNN), F32),
                        pltpu.VMEM((tk, D_RNN), F32),
                        pltpu.VMEM((tk, D_RNN), F32)],
        compiler_params=pltpu.CompilerParams(
            dimension_semantics=("arbitrary", "arbitrary"), vmem_limit_bytes=VMEM_LIMIT_BYTES),
        name="lru_mixer",
    )(xr, gr, conv_w, conv_b, wrg, brg, wig, big, lam)


def _attn_kernel(q_ref, k_ref, v_ref, o_ref, lse_ref, *, n_res, n_blk, res_unroll, blk_unroll):
    qi = lax.broadcasted_iota(jnp.int32, (Q_BLOCK, 2 * Q_BLOCK), 0)
    kj = lax.broadcasted_iota(jnp.int32, (Q_BLOCK, 2 * Q_BLOCK), 1)
    band2 = (kj >= qi) & (kj <= qi + Q_BLOCK)
    band1 = (lax.broadcasted_iota(jnp.int32, (Q_BLOCK, Q_BLOCK), 1)
             <= lax.broadcasted_iota(jnp.int32, (Q_BLOCK, Q_BLOCK), 0))
    lane = lax.broadcasted_iota(jnp.int32, (Q_BLOCK, HEAD_DIM), 1)
    seg = HEAD_DIM // HEADS_PER_GROUP

    def block(r, q_rows, kv_rows, band):
        lse_tile = jnp.zeros((Q_BLOCK, HEAD_DIM), F32)
        for hd, cols in enumerate(_slabs(GROUP_WIDTH)):
            q = q_ref[r, q_rows, cols]
            k = k_ref[r, kv_rows, cols]
            v = v_ref[r, kv_rows, cols]
            s = lax.dot_general(q, k, (((1,), (1,)), ((), ())), preferred_element_type=F32)
            s = jnp.where(band, s, NEG)
            m = jnp.max(s, axis=-1, keepdims=True)
            p = jnp.exp(s - m)
            den = jnp.sum(p, axis=-1, keepdims=True)
            o = jnp.dot(p.astype(BF16), v, preferred_element_type=F32) / den
            o_ref[r, q_rows, cols] = o.astype(o_ref.dtype)
            lse_tile = jnp.where(lane // seg == hd, m + jnp.log(den), lse_tile)
        lse_ref[r, q_rows, :] = lse_tile

    def residue(r, _):
        first = pl.ds(0, Q_BLOCK)
        block(r, first, first, band1)
        if n_blk > 1:
            def body(n, _):
                q_rows = pl.ds(pl.multiple_of(n * Q_BLOCK, Q_BLOCK), Q_BLOCK)
                kv_rows = pl.ds(pl.multiple_of((n - 1) * Q_BLOCK, Q_BLOCK), 2 * Q_BLOCK)
                block(r, q_rows, kv_rows, band2)
                return 0
            lax.fori_loop(1, n_blk, body, 0, unroll=blk_unroll)
        return 0

    if n_res == 1:
        residue(0, 0)
    else:
        lax.fori_loop(0, n_res, residue, 0, unroll=res_unroll)


def _attention_group(q, k, v, res_unroll, blk_unroll):
    B, d, L, _ = q.shape
    qkv_spec = pl.BlockSpec((None, d, L, GROUP_WIDTH), lambda b: (b, 0, 0, 0))
    lse_spec = pl.BlockSpec((None, d, L, HEAD_DIM), lambda b: (b, 0, 0, 0))
    return pl.pallas_call(
        functools.partial(_attn_kernel, n_res=d, n_blk=L // Q_BLOCK,
                          res_unroll=res_unroll, blk_unroll=blk_unroll),
        grid=(B,),
        in_specs=[qkv_spec, qkv_spec, qkv_spec],
        out_specs=[qkv_spec, lse_spec],
        out_shape=[jax.ShapeDtypeStruct((B, d, L, GROUP_WIDTH), BF16),
                   jax.ShapeDtypeStruct((B, d, L, HEAD_DIM), F32)],
        compiler_params=pltpu.CompilerParams(
            dimension_semantics=("arbitrary",), vmem_limit_bytes=VMEM_LIMIT_BYTES),
        name=f"attention_d{d}",
    )(q, k, v)


FF_CHUNK = 2 * V7X_MXU_DIM


def _merge_ffn_kernel(x_ref, gated_ref, o1_ref, o2_ref, o3_ref, l1_ref, l2_ref, l3_ref, gates_ref,
                      wl_ref, wa_ref, wo_ref, gpost_ref, gpre_ref, wg_ref, wu_ref, wd_ref, gffn_ref,
                      out_ref, act_ref, o_scr, l_scr, *, tm):
    y_lru = jnp.dot(gated_ref[...], wl_ref[...], preferred_element_type=F32)

    for g, (d, o_ref, l_ref) in enumerate(zip(DILATIONS, (o1_ref, o2_ref, o3_ref), (l1_ref, l2_ref, l3_ref))):
        n = tm // d
        for r in range(d):
            rows = pl.ds(r, n, stride=d) if d > 1 else pl.ds(0, n)
            l_scr[g, rows, :] = l_ref[r]
            for c, sl in enumerate(_slabs(GROUP_WIDTH)):
                o_scr[g, c, rows, :] = o_ref[r, :, sl].astype(F32)

    l1, l2, l3 = l_scr[0], l_scr[1], l_scr[2]
    mx = jnp.maximum(jnp.maximum(l1, l2), l3)
    es = [jnp.exp(l1 - mx), jnp.exp(l2 - mx), jnp.exp(l3 - mx)]
    inv = 1.0 / (es[0] + es[1] + es[2])
    seg = HEAD_DIM // HEADS_PER_GROUP
    heads = []
    for hd in range(HEADS_PER_GROUP):
        acc = None
        for g in range(N_GROUPS):
            w = (es[g] * inv)[:, hd * seg:hd * seg + 1]
            term = w * o_scr[g, hd]
            acc = term if acc is None else acc + term
        heads.append(acc)
    o = jnp.concatenate(heads, axis=1).astype(BF16)
    y_attn = jnp.dot(o, wa_ref[...], preferred_element_type=F32)

    g_lru = _sigmoid(gates_ref[:, :D_MODEL].astype(F32))
    g_attn = _sigmoid(gates_ref[:, D_MODEL:].astype(F32))
    merged = (g_lru * y_lru + g_attn * y_attn).astype(BF16)
    mix = jnp.dot(merged, wo_ref[...], preferred_element_type=F32)
    x1 = x_ref[...] + _rms_norm(mix, gpost_ref[...])

    h = _rms_norm(x1, gpre_ref[...]).astype(BF16)
    for c in range(0, D_FF, FF_CHUNK):
        w = min(FF_CHUNK, D_FF - c)
        gate = jnp.dot(h, wg_ref[:, c:c + w], preferred_element_type=F32)
        up = jnp.dot(h, wu_ref[:, c:c + w], preferred_element_type=F32)
        act_ref[:, c:c + w] = (gate * _sigmoid(gate) * up).astype(BF16)
    f = jnp.dot(act_ref[...], wd_ref[...], preferred_element_type=F32)
    out_ref[...] = x1 + _rms_norm(f, gffn_ref[...])


def _merge_ffn(x2, gated, os, ls, gates, wl, wa, wo, gpost, gpre, wg, wu, wd, gffn, S, tm):
    T = x2.shape[0]
    nt = S // tm
    row = lambda w: pl.BlockSpec((tm, w), lambda i: (i, 0))
    vec = _const_spec((1, D_MODEL))
    return pl.pallas_call(
        functools.partial(_merge_ffn_kernel, tm=tm),
        grid=(T // tm,),
        in_specs=[row(D_MODEL), row(D_RNN)]
                 + [_grouped_spec(d, tm, GROUP_WIDTH, nt) for d in DILATIONS]
                 + [_grouped_spec(d, tm, HEAD_DIM, nt) for d in DILATIONS]
                 + [row(2 * D_MODEL),
                    _const_spec((D_RNN, D_MODEL)), _const_spec((GROUP_WIDTH, D_MODEL)),
                    _const_spec((D_MODEL, D_MODEL)), vec, vec,
                    _const_spec((D_MODEL, D_FF)), _const_spec((D_MODEL, D_FF)),
                    _const_spec((D_FF, D_MODEL)), vec],
        out_specs=row(D_MODEL),
        out_shape=jax.ShapeDtypeStruct((T, D_MODEL), F32),
        scratch_shapes=[pltpu.VMEM((tm, D_FF), BF16),
                        pltpu.VMEM((N_GROUPS, HEADS_PER_GROUP, tm, HEAD_DIM), F32),
                        pltpu.VMEM((N_GROUPS, tm, HEAD_DIM), F32)],
        compiler_params=pltpu.CompilerParams(
            dimension_semantics=("arbitrary",), vmem_limit_bytes=VMEM_LIMIT_BYTES),
        name="merge_ffn",
    )(x2, gated, *os, *ls, gates, wl, wa, wo, gpost, gpre, wg, wu, wd, gffn)


def _pack_lru_gate(w):
    w4 = w.reshape(N_LRU_TILES, LRU_PACK, LRU_BLOCK, LRU_BLOCK)
    eye = jnp.eye(LRU_PACK, dtype=w.dtype)
    packed = jnp.einsum('jacd,ab->jacbd', w4, eye)
    return packed.reshape(N_LRU_TILES, V7X_MXU_DIM, V7X_MXU_DIM).astype(BF16)


ATTN_UNROLL = {1: (1, 3), 4: (1, 3), 16: (4, 1)}


def kernel(x, positions, pre_mix_norm, w_in, conv_w, conv_b, w_rg, b_rg, w_ig, b_ig, lru_lambda,
           w_lru_proj, w_attn_proj, w_out, post_mix_norm, pre_ffn_norm, w_ffn_gate, w_ffn_up,
           w_ffn_down, post_ffn_norm):
    B, S, D = x.shape
    assert D == D_MODEL and S % TM == 0 and S % TK == 0 and TM % (DILATIONS[-1] * 2 * V7X_SUBLANES) == 0
    assert pre_mix_norm.shape[0] == 1, "single-layer block"
    T = B * S

    inv_freq = ROPE_THETA ** (-jnp.arange(0, ROPE_DIM, 2, dtype=F32) / ROPE_DIM)
    freq = jnp.concatenate([inv_freq, inv_freq, jnp.zeros((HEAD_DIM - ROPE_DIM,), F32)])[None, :]

    x2 = x.reshape(T, D)
    pos2 = positions.reshape(T, 1)
    row = lambda p: p[0][None, :]

    xr, gr, q1, q2, q3, k1, k2, k3, v1, v2, v3, gates = _in_proj(
        x2, pos2, row(pre_mix_norm), freq, w_in[0].astype(BF16), B, S, TM)

    gated = _lru_mixer(xr.reshape(B, S, D_RNN), gr.reshape(B, S, D_RNN), conv_w[0], row(conv_b),
                       _pack_lru_gate(w_rg[0]), row(b_rg), _pack_lru_gate(w_ig[0]), row(b_ig),
                       row(lru_lambda), TK)

    os, ls = [], []
    for (q, k, v), d in zip(((q1, k1, v1), (q2, k2, v2), (q3, k3, v3)), DILATIONS):
        o, l = _attention_group(q, k, v, *ATTN_UNROLL[d])
        os.append(o)
        ls.append(l)

    out = _merge_ffn(x2, gated.reshape(T, D_RNN), os, ls, gates,
                     w_lru_proj[0].astype(BF16), w_attn_proj[0].astype(BF16), w_out[0].astype(BF16),
                     row(post_mix_norm), row(pre_ffn_norm), w_ffn_gate[0].astype(BF16),
                     w_ffn_up[0].astype(BF16), w_ffn_down[0].astype(BF16), row(post_ffn_norm), S, TM)
    return out.reshape(B, S, D)
```

```python
import functools
import math

import jax
import jax.numpy as jnp
from jax import lax
from jax.experimental import pallas as pl
from jax.experimental.pallas import tpu as pltpu

D_MODEL = 1024
D_RNN = 1024
N_LRU_BLOCKS = 16
LRU_BLOCK = D_RNN // N_LRU_BLOCKS
CONV_WIDTH = 4
LRU_C = 8.0
ATTN_GROUPS = ((128, 1), (512, 4), (2048, 16))
DILATIONS = tuple(d for _, d in ATTN_GROUPS)
N_GROUPS = len(ATTN_GROUPS)
HEADS_PER_GROUP = 4
HEAD_DIM = 128
GROUP_WIDTH = HEADS_PER_GROUP * HEAD_DIM
ATTN_WIDTH = N_GROUPS * GROUP_WIDTH
ROPE_DIM = HEAD_DIM // 4
ROPE_THETA = 500000.0
Q_BLOCK = 128
D_FF = ((8 * D_MODEL // 3 + 255) // 256) * 256
IN_WIDTH = 2 * D_RNN + 3 * ATTN_WIDTH + 2 * D_MODEL
EPS = 1e-6
NEG = -1e30

V7X_LANES = 128
V7X_SUBLANES = 8
V7X_MXU_DIM = 256
V7X_VMEM_BYTES = 64 * 1024 * 1024
VMEM_LIMIT_BYTES = V7X_VMEM_BYTES - 8 * 1024 * 1024

BF16 = jnp.bfloat16
F32 = jnp.float32

LRU_PACK = V7X_MXU_DIM // LRU_BLOCK
N_LRU_TILES = N_LRU_BLOCKS // LRU_PACK

TM = 512
SUB_ROWS = 256
TK = 256

assert all(w // d == Q_BLOCK for w, d in ATTN_GROUPS), "band logic assumes window == dilation * Q_BLOCK"


def _rms_norm(x, g):
    return x * lax.rsqrt(jnp.mean(x * x, axis=-1, keepdims=True) + EPS) * g


def _sigmoid(x):
    return 0.5 * jnp.tanh(0.5 * x) + 0.5


def _gelu_tanh(x):
    c = math.sqrt(2.0 / math.pi)
    return 0.5 * x * (1.0 + jnp.tanh(c * (x + 0.044715 * (x * x * x))))


def _const_spec(shape):
    nd = len(shape)
    return pl.BlockSpec(shape, lambda *_: (0,) * nd, pipeline_mode=pl.Buffered(1))


def _slabs(width):
    return [slice(c * V7X_LANES, (c + 1) * V7X_LANES) for c in range(width // V7X_LANES)]


def _residue_major(slab_ref, d, n_rows):
    if d == 1:
        return jnp.concatenate([slab_ref[c] for c in range(slab_ref.shape[0])], axis=1)
    n = n_rows // d
    return jnp.concatenate(
        [jnp.concatenate([slab_ref[c, pl.ds(r, n, stride=d), :] for c in range(slab_ref.shape[0])], axis=1)
         for r in range(d)], axis=0)


IN_CHUNK = GROUP_WIDTH


def _in_proj_kernel(x_ref, pos_ref, g_ref, freq_ref, w_ref,
                    xr_ref, gr_ref, q1_ref, q2_ref, q3_ref, k1_ref, k2_ref, k3_ref,
                    v1_ref, v2_ref, v3_ref, gates_ref, h_scr, rope_scr, *, tm):
    half = ROPE_DIM // 2
    scale = HEAD_DIM ** -0.5
    q_base = 2 * D_RNN // IN_CHUNK
    gates_base = q_base + 3 * N_GROUPS

    def rope(y, tabs, mul):
        c, s_lo, s_hi = (tabs[:, sl] if mul == 1.0 else tabs[:, sl] * mul for sl in _slabs(3 * HEAD_DIM))
        outs = []
        for sl in _slabs(GROUP_WIDTH):
            t = y[:, sl]
            up = pltpu.roll(t, HEAD_DIM - half, axis=1)
            dn = pltpu.roll(t, half, axis=1)
            outs.append(t * c + up * s_lo + dn * s_hi)
        return jnp.concatenate(outs, axis=1)

    def chunk(lhs, ci):
        return jnp.dot(lhs, w_ref[:, ci * IN_CHUNK:(ci + 1) * IN_CHUNK], preferred_element_type=F32)

    for sub in range(tm // SUB_ROWS):
        rows = slice(sub * SUB_ROWS, (sub + 1) * SUB_ROWS)
        h_sub, rope_sub = h_scr.at[sub], rope_scr.at[sub]
        h = _rms_norm(x_ref[rows, :], g_ref[...])
        for c, sl in enumerate(_slabs(D_MODEL)):
            h_sub[c] = h[:, sl]

        ang = pos_ref[rows, :].astype(F32) * freq_ref[...]
        sin = jnp.sin(ang)
        lane = lax.broadcasted_iota(jnp.int32, ang.shape, 1)
        rope_sub[0] = jnp.cos(ang)
        rope_sub[1] = jnp.where(lane < half, -sin, 0.0)
        rope_sub[2] = jnp.where(lane >= half, sin, 0.0)

        def emit(ref, y, d):
            n = SUB_ROWS // d
            for r in range(d):
                ref[r, sub * n:(sub + 1) * n, :] = y[r * n:(r + 1) * n, :].astype(ref.dtype)

        for g, d in enumerate(DILATIONS):
            hp = _residue_major(h_sub, d, SUB_ROWS).astype(BF16)
            tabs = _residue_major(rope_sub, d, SUB_ROWS)
            emit((q1_ref, q2_ref, q3_ref)[g], rope(chunk(hp, q_base + g), tabs, scale), d)
            emit((k1_ref, k2_ref, k3_ref)[g], rope(chunk(hp, q_base + N_GROUPS + g), tabs, 1.0), d)
            emit((v1_ref, v2_ref, v3_ref)[g], chunk(hp, q_base + 2 * N_GROUPS + g), d)
            if d == 1:
                for j in range(D_RNN // IN_CHUNK):
                    cols = slice(j * IN_CHUNK, (j + 1) * IN_CHUNK)
                    xr_ref[rows, cols] = chunk(hp, j).astype(xr_ref.dtype)
                    gr_ref[rows, cols] = chunk(hp, D_RNN // IN_CHUNK + j).astype(gr_ref.dtype)
                for j in range(2 * D_MODEL // IN_CHUNK):
                    cols = slice(j * IN_CHUNK, (j + 1) * IN_CHUNK)
                    gates_ref[rows, cols] = chunk(hp, gates_base + j).astype(gates_ref.dtype)


def _grouped_spec(d, rows, width, steps_per_seq):
    return pl.BlockSpec((None, d, rows // d, width),
                        lambda i: (i // steps_per_seq, 0, i % steps_per_seq, 0))


def _in_proj(x2, pos2, g, freq, w_in, B, S, tm):
    T = B * S
    nt = S // tm
    row = lambda w: pl.BlockSpec((tm, w), lambda i: (i, 0))
    qkv_specs = [_grouped_spec(d, tm, GROUP_WIDTH, nt) for d in DILATIONS] * 3
    qkv_shapes = [jax.ShapeDtypeStruct((B, d, S // d, GROUP_WIDTH), BF16) for d in DILATIONS] * 3
    flat = lambda w: jax.ShapeDtypeStruct((T, w), BF16)
    return pl.pallas_call(
        functools.partial(_in_proj_kernel, tm=tm),
        grid=(T // tm,),
        in_specs=[row(D_MODEL), row(1), _const_spec((1, D_MODEL)), _const_spec((1, HEAD_DIM)),
                  _const_spec((D_MODEL, IN_WIDTH))],
        out_specs=[row(D_RNN), row(D_RNN)] + qkv_specs + [row(2 * D_MODEL)],
        out_shape=[flat(D_RNN), flat(D_RNN)] + qkv_shapes + [flat(2 * D_MODEL)],
        scratch_shapes=[pltpu.VMEM((tm // SUB_ROWS, D_MODEL // V7X_LANES, SUB_ROWS, V7X_LANES), F32),
                        pltpu.VMEM((tm // SUB_ROWS, 3, SUB_ROWS, V7X_LANES), F32)],
        compiler_params=pltpu.CompilerParams(
            dimension_semantics=("arbitrary",), vmem_limit_bytes=VMEM_LIMIT_BYTES),
        name="in_proj",
    )(x2, pos2, g, freq, w_in)


def _lru_kernel(xr_ref, gr_ref, cw_ref, cb_ref, wrg_ref, brg_ref, wig_ref, big_ref, lam_ref,
                out_ref, tail_ref, carry_ref, a_ref, u_ref, h_ref, *, tk):
    t_idx = pl.program_id(1)

    @pl.when(t_idx == 0)
    def _():
        tail_ref[...] = jnp.zeros_like(tail_ref)
        carry_ref[...] = jnp.zeros_like(carry_ref)

    x = xr_ref[...].astype(F32)
    tail = tail_ref[...]
    tail_ref[...] = x[tk - V7X_SUBLANES:, :]
    row8 = lax.broadcasted_iota(jnp.int32, (V7X_SUBLANES, D_RNN), 0)

    y = x * cw_ref[CONV_WIDTH - 1:CONV_WIDTH, :] + cb_ref[...]
    for s in range(1, CONV_WIDTH):
        xs = pltpu.roll(x, s, axis=0)
        head = jnp.where(row8 < s, pltpu.roll(tail, s, axis=0), xs[:V7X_SUBLANES, :])
        xs = jnp.concatenate([head, xs[V7X_SUBLANES:, :]], axis=0)
        y = y + xs * cw_ref[CONV_WIDTH - 1 - s:CONV_WIDTH - s, :]

    yb = y.astype(BF16)
    r_parts, i_parts = [], []
    for j in range(N_LRU_TILES):
        sl = slice(j * V7X_MXU_DIM, (j + 1) * V7X_MXU_DIM)
        r_parts.append(jnp.dot(yb[:, sl], wrg_ref[j], preferred_element_type=F32))
        i_parts.append(jnp.dot(yb[:, sl], wig_ref[j], preferred_element_type=F32))
    r = _sigmoid(jnp.concatenate(r_parts, axis=1) + brg_ref[...])
    ig = _sigmoid(jnp.concatenate(i_parts, axis=1) + big_ref[...])

    nl = -lam_ref[...]
    softplus = jnp.maximum(nl, 0.0) + jnp.log1p(jnp.exp(-jnp.abs(nl)))
    log_a = (-LRU_C) * r * softplus
    a = jnp.exp(log_a)
    mult = jnp.sqrt(1.0 - a * a)
    pos = t_idx * tk + lax.broadcasted_iota(jnp.int32, (tk, 1), 0)
    mult = jnp.where(pos == 0, 1.0, mult)
    a_ref[...] = a
    u_ref[...] = mult * (ig * y)

    def group(j, carry):
        rows = pl.ds(pl.multiple_of(j * V7X_SUBLANES, V7X_SUBLANES), V7X_SUBLANES)
        a8 = a_ref[rows, :]
        u8 = u_ref[rows, :]
        for s in (1, 2, 4):
            keep = row8 >= s
            a_m = jnp.where(keep, a8, 0.0)
            u8 = u8 + a_m * pltpu.roll(u8, s, axis=0)
            a8 = jnp.where(keep, a8 * pltpu.roll(a8, s, axis=0), a8)
        h8 = u8 + a8 * carry
        h_ref[rows, :] = h8
        return jnp.broadcast_to(h8[V7X_SUBLANES - 1:V7X_SUBLANES, :], (V7X_SUBLANES, D_RNN))

    carry_ref[...] = lax.fori_loop(0, tk // V7X_SUBLANES, group, carry_ref[...], unroll=4)

    out_ref[...] = (h_ref[...] * _gelu_tanh(gr_ref[...].astype(F32))).astype(out_ref.dtype)


def _lru_mixer(xr, gr, conv_w, conv_b, wrg, brg, wig, big, lam, tk):
    B, S, _ = xr.shape
    tile = pl.BlockSpec((None, tk, D_RNN), lambda b, t: (b, t, 0))
    vec = _const_spec((1, D_RNN))
    gate_w = _const_spec((N_LRU_TILES, V7X_MXU_DIM, V7X_MXU_DIM))
    return pl.pallas_call(
        functools.partial(_lru_kernel, tk=tk),
        grid=(B, S // tk),
        in_specs=[tile, tile, _const_spec((CONV_WIDTH, D_RNN)), vec, gate_w, vec, gate_w, vec, vec],
        out_specs=tile,
        out_shape=jax.ShapeDtypeStruct((B, S, D_RNN), BF16),
        scratch_shapes=[pltpu.VMEM((V7X_SUBLANES, D_RNN), F32),
                        pltpu.VMEM((V7X_SUBLANES, D_RNN), F32),
                        pltpu.VMEM((tk, D_RNN), F32),
                        pltpu.VMEM((tk, D_RNN), F32),
                        pltpu.VMEM((tk, D_RNN), F32)],
        compiler_params=pltpu.CompilerParams(
            dimension_semantics=("arbitrary", "arbitrary"), vmem_limit_bytes=VMEM_LIMIT_BYTES),
        name="lru_mixer",
    )(xr, gr, conv_w, conv_b, wrg, brg, wig, big, lam)


def _attn_kernel(q_ref, k_ref, v_ref, o_ref, lse_ref, *, n_res, n_blk):
    qi = lax.broadcasted_iota(jnp.int32, (Q_BLOCK, 2 * Q_BLOCK), 0)
    kj = lax.broadcasted_iota(jnp.int32, (Q_BLOCK, 2 * Q_BLOCK), 1)
    band2 = (kj >= qi) & (kj <= qi + Q_BLOCK)
    band1 = (lax.broadcasted_iota(jnp.int32, (Q_BLOCK, Q_BLOCK), 1)
             <= lax.broadcasted_iota(jnp.int32, (Q_BLOCK, Q_BLOCK), 0))
    lane = lax.broadcasted_iota(jnp.int32, (Q_BLOCK, HEAD_DIM), 1)
    seg = HEAD_DIM // HEADS_PER_GROUP

    def rows_of(n):
        q_rows = slice(n * Q_BLOCK, (n + 1) * Q_BLOCK)
        kv_rows = slice(max(n - 1, 0) * Q_BLOCK, (n + 1) * Q_BLOCK)
        return q_rows, kv_rows

    def scores(r, n):
        q_rows, kv_rows = rows_of(n)
        band = band1 if n == 0 else band2
        out = []
        for cols in _slabs(GROUP_WIDTH):
            s = lax.dot_general(q_ref[r, q_rows, cols], k_ref[r, kv_rows, cols],
                                (((1,), (1,)), ((), ())), preferred_element_type=F32)
            out.append(jnp.where(band, s, NEG))
        return out

    def finish(r, n, s_list):
        q_rows, kv_rows = rows_of(n)
        ps, dens = [], []
        lse_tile = jnp.zeros((Q_BLOCK, HEAD_DIM), F32)
        for hd, s in enumerate(s_list):
            m = jnp.max(s, axis=-1, keepdims=True)
            p = jnp.exp(s - m)
            den = jnp.sum(p, axis=-1, keepdims=True)
            ps.append(p.astype(BF16))
            dens.append(den)
            lse_tile = jnp.where(lane // seg == hd, m + jnp.log(den), lse_tile)
        lse_ref[r, q_rows, :] = lse_tile
        for hd, cols in enumerate(_slabs(GROUP_WIDTH)):
            o = jnp.dot(ps[hd], v_ref[r, kv_rows, cols], preferred_element_type=F32) / dens[hd]
            o_ref[r, q_rows, cols] = o.astype(o_ref.dtype)

    blocks = [(r, n) for r in range(n_res) for n in range(n_blk)]
    pending = None
    for blk in blocks:
        s_list = scores(*blk)
        if pending is not None:
            finish(*pending)
        pending = (*blk, s_list)
    finish(*pending)


def _attention_group(q, k, v):
    B, d, L, _ = q.shape
    qkv_spec = pl.BlockSpec((None, d, L, GROUP_WIDTH), lambda b: (b, 0, 0, 0))
    lse_spec = pl.BlockSpec((None, d, L, HEAD_DIM), lambda b: (b, 0, 0, 0))
    return pl.pallas_call(
        functools.partial(_attn_kernel, n_res=d, n_blk=L // Q_BLOCK),
        grid=(B,),
        in_specs=[qkv_spec, qkv_spec, qkv_spec],
        out_specs=[qkv_spec, lse_spec],
        out_shape=[jax.ShapeDtypeStruct((B, d, L, GROUP_WIDTH), BF16),
                   jax.ShapeDtypeStruct((B, d, L, HEAD_DIM), F32)],
        compiler_params=pltpu.CompilerParams(
            dimension_semantics=("arbitrary",), vmem_limit_bytes=VMEM_LIMIT_BYTES),
        name=f"attention_d{d}",
    )(q, k, v)


FF_CHUNK = 2 * V7X_MXU_DIM


def _merge_ffn_kernel(x_ref, gated_ref, o1_ref, o2_ref, o3_ref, l1_ref, l2_ref, l3_ref, gates_ref,
                      wl_ref, wa_ref, wo_ref, gpost_ref, gpre_ref, wg_ref, wu_ref, wd_ref, gffn_ref,
                      out_ref, act_ref, o_scr, l_scr, x1_scr, *, tm):
    seg = HEAD_DIM // HEADS_PER_GROUP

    def mix_phase(sub):
        rows = slice(sub * SUB_ROWS, (sub + 1) * SUB_ROWS)
        y_lru = jnp.dot(gated_ref[rows, :], wl_ref[...], preferred_element_type=F32)

        for g, (d, o_ref, l_ref) in enumerate(zip(DILATIONS, (o1_ref, o2_ref, o3_ref), (l1_ref, l2_ref, l3_ref))):
            n = SUB_ROWS // d
            src = slice(sub * n, (sub + 1) * n)
            for r in range(d):
                dst = pl.ds(r, n, stride=d) if d > 1 else pl.ds(0, n)
                l_scr[sub, g, dst, :] = l_ref[r, src, :]
                for c, sl in enumerate(_slabs(GROUP_WIDTH)):
                    o_scr[sub, g, c, dst, :] = o_ref[r, src, sl].astype(F32)

        ls = [l_scr[sub, g] for g in range(N_GROUPS)]
        mx = jnp.maximum(jnp.maximum(ls[0], ls[1]), ls[2])
        es = [jnp.exp(l - mx) for l in ls]
        inv = 1.0 / (es[0] + es[1] + es[2])
        heads = []
        for hd in range(HEADS_PER_GROUP):
            acc = None
            for g in range(N_GROUPS):
                w = (es[g] * inv)[:, hd * seg:hd * seg + 1]
                term = w * o_scr[sub, g, hd]
                acc = term if acc is None else acc + term
            heads.append(acc)
        o = jnp.concatenate(heads, axis=1).astype(BF16)
        y_attn = jnp.dot(o, wa_ref[...], preferred_element_type=F32)

        g_lru = _sigmoid(gates_ref[rows, :D_MODEL].astype(F32))
        g_attn = _sigmoid(gates_ref[rows, D_MODEL:].astype(F32))
        merged = (g_lru * y_lru + g_attn * y_attn).astype(BF16)
        mix = jnp.dot(merged, wo_ref[...], preferred_element_type=F32)
        x1 = x_ref[rows, :] + _rms_norm(mix, gpost_ref[...])
        x1_scr[sub] = x1
        return _rms_norm(x1, gpre_ref[...]).astype(BF16)

    def ffn_phase(sub, h):
        rows = slice(sub * SUB_ROWS, (sub + 1) * SUB_ROWS)
        for c in range(0, D_FF, FF_CHUNK):
            w = min(FF_CHUNK, D_FF - c)
            gate = jnp.dot(h, wg_ref[:, c:c + w], preferred_element_type=F32)
            up = jnp.dot(h, wu_ref[:, c:c + w], preferred_element_type=F32)
            act_ref[sub, :, c:c + w] = (gate * _sigmoid(gate) * up).astype(BF16)
        f = jnp.dot(act_ref[sub], wd_ref[...], preferred_element_type=F32)
        out_ref[rows, :] = x1_scr[sub] + _rms_norm(f, gffn_ref[...])

    n_sub = tm // SUB_ROWS
    hs = [mix_phase(sub) for sub in range(n_sub)]
    for sub in range(n_sub):
        ffn_phase(sub, hs[sub])


def _merge_ffn(x2, gated, os, ls, gates, wl, wa, wo, gpost, gpre, wg, wu, wd, gffn, S, tm):
    T = x2.shape[0]
    nt = S // tm
    row = lambda w: pl.BlockSpec((tm, w), lambda i: (i, 0))
    vec = _const_spec((1, D_MODEL))
    return pl.pallas_call(
        functools.partial(_merge_ffn_kernel, tm=tm),
        grid=(T // tm,),
        in_specs=[row(D_MODEL), row(D_RNN)]
                 + [_grouped_spec(d, tm, GROUP_WIDTH, nt) for d in DILATIONS]
                 + [_grouped_spec(d, tm, HEAD_DIM, nt) for d in DILATIONS]
                 + [row(2 * D_MODEL),
                    _const_spec((D_RNN, D_MODEL)), _const_spec((GROUP_WIDTH, D_MODEL)),
                    _const_spec((D_MODEL, D_MODEL)), vec, vec,
                    _const_spec((D_MODEL, D_FF)), _const_spec((D_MODEL, D_FF)),
                    _const_spec((D_FF, D_MODEL)), vec],
        out_specs=row(D_MODEL),
        out_shape=jax.ShapeDtypeStruct((T, D_MODEL), F32),
        scratch_shapes=[pltpu.VMEM((tm // SUB_ROWS, SUB_ROWS, D_FF), BF16),
                        pltpu.VMEM((tm // SUB_ROWS, N_GROUPS, HEADS_PER_GROUP, SUB_ROWS, HEAD_DIM), F32),
                        pltpu.VMEM((tm // SUB_ROWS, N_GROUPS, SUB_ROWS, HEAD_DIM), F32),
                        pltpu.VMEM((tm // SUB_ROWS, SUB_ROWS, D_MODEL), F32)],
        compiler_params=pltpu.CompilerParams(
            dimension_semantics=("arbitrary",), vmem_limit_bytes=VMEM_LIMIT_BYTES),
        name="merge_ffn",
    )(x2, gated, *os, *ls, gates, wl, wa, wo, gpost, gpre, wg, wu, wd, gffn)


def _pack_lru_gate(w):
    w4 = w.reshape(N_LRU_TILES, LRU_PACK, LRU_BLOCK, LRU_BLOCK)
    eye = jnp.eye(LRU_PACK, dtype=w.dtype)
    packed = jnp.einsum('jacd,ab->jacbd', w4, eye)
    return packed.reshape(N_LRU_TILES, V7X_MXU_DIM, V7X_MXU_DIM).astype(BF16)


def kernel(x, positions, pre_mix_norm, w_in, conv_w, conv_b, w_rg, b_rg, w_ig, b_ig, lru_lambda,
           w_lru_proj, w_attn_proj, w_out, post_mix_norm, pre_ffn_norm, w_ffn_gate, w_ffn_up,
           w_ffn_down, post_ffn_norm):
    B, S, D = x.shape
    assert D == D_MODEL and S % TM == 0 and S % TK == 0 and TM % SUB_ROWS == 0
    assert SUB_ROWS % (DILATIONS[-1] * 2 * V7X_SUBLANES) == 0
    assert pre_mix_norm.shape[0] == 1, "single-layer block"
    T = B * S

    inv_freq = ROPE_THETA ** (-jnp.arange(0, ROPE_DIM, 2, dtype=F32) / ROPE_DIM)
    freq = jnp.concatenate([inv_freq, inv_freq, jnp.zeros((HEAD_DIM - ROPE_DIM,), F32)])[None, :]

    x2 = x.reshape(T, D)
    pos2 = positions.reshape(T, 1)
    row = lambda p: p[0][None, :]

    xr, gr, q1, q2, q3, k1, k2, k3, v1, v2, v3, gates = _in_proj(
        x2, pos2, row(pre_mix_norm), freq, w_in[0].astype(BF16), B, S, TM)

    gated = _lru_mixer(xr.reshape(B, S, D_RNN), gr.reshape(B, S, D_RNN), conv_w[0], row(conv_b),
                       _pack_lru_gate(w_rg[0]), row(b_rg), _pack_lru_gate(w_ig[0]), row(b_ig),
                       row(lru_lambda), TK)

    os, ls = [], []
    for q, k, v in ((q1, k1, v1), (q2, k2, v2), (q3, k3, v3)):
        o, l = _attention_group(q, k, v)
        os.append(o)
        ls.append(l)

    out = _merge_ffn(x2, gated.reshape(T, D_RNN), os, ls, gates,
                     w_lru_proj[0].astype(BF16), w_attn_proj[0].astype(BF16), w_out[0].astype(BF16),
                     row(post_mix_norm), row(pre_ffn_norm), w_ffn_gate[0].astype(BF16),
                     w_ffn_up[0].astype(BF16), w_ffn_down[0].astype(BF16), row(post_ffn_norm), S, TM)
    return out.reshape(B, S, D)
```

```python
import functools
import math

import jax
import jax.numpy as jnp
from jax import lax
from jax.experimental import pallas as pl
from jax.experimental.pallas import tpu as pltpu

D_MODEL = 1024
D_RNN = 1024
N_LRU_BLOCKS = 16
LRU_BLOCK = D_RNN // N_LRU_BLOCKS
CONV_WIDTH = 4
LRU_C = 8.0
ATTN_GROUPS = ((128, 1), (512, 4), (2048, 16))
DILATIONS = tuple(d for _, d in ATTN_GROUPS)
N_GROUPS = len(ATTN_GROUPS)
HEADS_PER_GROUP = 4
HEAD_DIM = 128
GROUP_WIDTH = HEADS_PER_GROUP * HEAD_DIM
ATTN_WIDTH = N_GROUPS * GROUP_WIDTH
ROPE_DIM = HEAD_DIM // 4
ROPE_THETA = 500000.0
Q_BLOCK = 128
D_FF = ((8 * D_MODEL // 3 + 255) // 256) * 256
IN_WIDTH = 2 * D_RNN + 3 * ATTN_WIDTH + 2 * D_MODEL
EPS = 1e-6
NEG = -1e30

V7X_LANES = 128
V7X_SUBLANES = 8
V7X_MXU_DIM = 256
V7X_VMEM_BYTES = 64 * 1024 * 1024
VMEM_LIMIT_BYTES = V7X_VMEM_BYTES - 8 * 1024 * 1024

BF16 = jnp.bfloat16
F32 = jnp.float32

LRU_PACK = V7X_MXU_DIM // LRU_BLOCK
N_LRU_TILES = N_LRU_BLOCKS // LRU_PACK

TM = 512
SUB_ROWS = 256
N_CHUNKS = V7X_SUBLANES
CHUNK_LEN = SUB_ROWS // N_CHUNKS
CONV_TAIL = (CONV_WIDTH - 1) * V7X_SUBLANES

assert all(w // d == Q_BLOCK for w, d in ATTN_GROUPS), "band logic assumes window == dilation * Q_BLOCK"
assert CONV_WIDTH - 1 <= CHUNK_LEN


def _rms_norm(x, g):
    return x * lax.rsqrt(jnp.mean(x * x, axis=-1, keepdims=True) + EPS) * g


def _sigmoid(x):
    return 0.5 * jnp.tanh(0.5 * x) + 0.5


def _gelu_tanh(x):
    c = math.sqrt(2.0 / math.pi)
    return x * (0.5 * jnp.tanh(x * (c + (c * 0.044715) * (x * x))) + 0.5)


def _const_spec(shape):
    nd = len(shape)
    return pl.BlockSpec(shape, lambda *_: (0,) * nd, pipeline_mode=pl.Buffered(1))


def _slabs(width):
    return [slice(c * V7X_LANES, (c + 1) * V7X_LANES) for c in range(width // V7X_LANES)]


def _residue_major(slab_ref, d, n_rows):
    if d == 1:
        return jnp.concatenate([slab_ref[c] for c in range(slab_ref.shape[0])], axis=1)
    n = n_rows // d
    return jnp.concatenate(
        [jnp.concatenate([slab_ref[c, pl.ds(r, n, stride=d), :] for c in range(slab_ref.shape[0])], axis=1)
         for r in range(d)], axis=0)


IN_CHUNK = GROUP_WIDTH


def _in_proj_lru_kernel(x_ref, pos_ref, g_ref, freq_ref, w_ref,
                        cw_ref, cb_ref, wrg_ref, brg_ref, wig_ref, big_ref, lam_ref,
                        gated_ref, q1_ref, q2_ref, q3_ref, k1_ref, k2_ref, k3_ref,
                        v1_ref, v2_ref, v3_ref, gates_ref,
                        h_scr, rope_scr, hl_scr, xr_halo, hstate, g_scr, *, tm, steps_per_seq):
    half = ROPE_DIM // 2
    scale = HEAD_DIM ** -0.5
    q_base = 2 * D_RNN // IN_CHUNK
    gates_base = q_base + 3 * N_GROUPS
    seq_pos = (pl.program_id(0) % steps_per_seq) * tm
    sub8 = lax.broadcasted_iota(jnp.int32, (V7X_SUBLANES, D_RNN), 0)
    slabs = _slabs(D_MODEL)

    def rope(y, tabs, mul):
        c, s_lo, s_hi = (tabs[:, sl] if mul == 1.0 else tabs[:, sl] * mul for sl in _slabs(3 * HEAD_DIM))
        outs = []
        for sl in _slabs(GROUP_WIDTH):
            t = y[:, sl]
            up = pltpu.roll(t, HEAD_DIM - half, axis=1)
            dn = pltpu.roll(t, half, axis=1)
            outs.append(t * c + up * s_lo + dn * s_hi)
        return jnp.concatenate(outs, axis=1)

    def chunk(lhs, ci):
        return jnp.dot(lhs, w_ref[:, ci * IN_CHUNK:(ci + 1) * IN_CHUNK], preferred_element_type=F32)

    def wide(lhs, first_chunk):
        return jnp.concatenate([chunk(lhs, first_chunk + j) for j in range(D_RNN // IN_CHUNK)], axis=1)

    nl = -lam_ref[...]
    softplus = jnp.maximum(nl, 0.0) + jnp.log1p(jnp.exp(-jnp.abs(nl)))
    log_a_coef = (-LRU_C) * softplus

    @pl.when(seq_pos == 0)
    def _():
        xr_halo[...] = jnp.zeros(xr_halo.shape, F32)
        hstate[...] = jnp.zeros(hstate.shape, F32)

    for sub in range(tm // SUB_ROWS):
        rows = slice(sub * SUB_ROWS, (sub + 1) * SUB_ROWS)
        h_sub, rope_sub = h_scr.at[sub], rope_scr.at[sub]
        h = _rms_norm(x_ref[rows, :], g_ref[...])
        for c, sl in enumerate(slabs):
            h_sub[c] = h[:, sl]
        for c in range(N_CHUNKS):
            for si, sl in enumerate(slabs):
                hl_scr[si, pl.ds(c, CHUNK_LEN, stride=N_CHUNKS), :] = h[c * CHUNK_LEN:(c + 1) * CHUNK_LEN, sl]

        ang = pos_ref[rows, :].astype(F32) * freq_ref[...]
        sin = jnp.sin(ang)
        lane = lax.broadcasted_iota(jnp.int32, ang.shape, 1)
        rope_sub[0] = jnp.cos(ang)
        rope_sub[1] = jnp.where(lane < half, -sin, 0.0)
        rope_sub[2] = jnp.where(lane >= half, sin, 0.0)

        def emit(ref, y, d):
            n = SUB_ROWS // d
            for r in range(d):
                ref[r, sub * n:(sub + 1) * n, :] = y[r * n:(r + 1) * n, :].astype(ref.dtype)

        def qkv(g, d, hp):
            tabs = _residue_major(rope_sub, d, SUB_ROWS)
            emit((q1_ref, q2_ref, q3_ref)[g], rope(chunk(hp, q_base + g), tabs, scale), d)
            emit((k1_ref, k2_ref, k3_ref)[g], rope(chunk(hp, q_base + N_GROUPS + g), tabs, 1.0), d)
            emit((v1_ref, v2_ref, v3_ref)[g], chunk(hp, q_base + 2 * N_GROUPS + g), d)

        hl = jnp.concatenate([hl_scr[si] for si in range(len(slabs))], axis=1).astype(BF16)
        xr = wide(hl, 0)
        gr = wide(hl, D_RNN // IN_CHUNK)
        hp = _residue_major(h_sub, 1, SUB_ROWS).astype(BF16)
        qkv(0, DILATIONS[0], hp)

        prev = xr_halo[...]
        wraps = []
        for j in range(CONV_WIDTH - 1):
            grp = slice(j * V7X_SUBLANES, (j + 1) * V7X_SUBLANES)
            cur = xr[SUB_ROWS - CONV_TAIL + j * V7X_SUBLANES:SUB_ROWS - CONV_TAIL + (j + 1) * V7X_SUBLANES, :]
            wraps.append(jnp.where(sub8 == 0, pltpu.roll(prev[grp, :], 1, axis=0), pltpu.roll(cur, 1, axis=0)))
        xr_halo[...] = xr[SUB_ROWS - CONV_TAIL:, :]
        xe = jnp.concatenate(wraps + [xr], axis=0)
        y = cb_ref[...]
        for s in range(CONV_WIDTH):
            lo = CONV_TAIL - s * V7X_SUBLANES
            y = y + xe[lo:lo + SUB_ROWS, :] * cw_ref[CONV_WIDTH - 1 - s:CONV_WIDTH - s, :]
        yb = y.astype(BF16)

        for j in range(2 * D_MODEL // IN_CHUNK):
            cols = slice(j * IN_CHUNK, (j + 1) * IN_CHUNK)
            gates_ref[rows, cols] = chunk(hp, gates_base + j).astype(gates_ref.dtype)
        qkv(1, DILATIONS[1], _residue_major(h_sub, DILATIONS[1], SUB_ROWS).astype(BF16))
        r_parts, i_parts = [], []
        for j in range(N_LRU_TILES):
            sl = slice(j * V7X_MXU_DIM, (j + 1) * V7X_MXU_DIM)
            r_parts.append(jnp.dot(yb[:, sl], wrg_ref[j], preferred_element_type=F32))
            i_parts.append(jnp.dot(yb[:, sl], wig_ref[j], preferred_element_type=F32))
        for g in range(2, N_GROUPS):
            qkv(g, DILATIONS[g], _residue_major(h_sub, DILATIONS[g], SUB_ROWS).astype(BF16))

        r = _sigmoid(jnp.concatenate(r_parts, axis=1) + brg_ref[...])
        ig = _sigmoid(jnp.concatenate(i_parts, axis=1) + big_ref[...])
        a = jnp.exp(log_a_coef * r)
        mult = jnp.sqrt(1.0 - a * a)
        if sub == 0:
            row = seq_pos + lax.broadcasted_iota(jnp.int32, (SUB_ROWS, 1), 0)
            mult = jnp.where(row == 0, 1.0, mult)
        u = mult * (ig * y)

        h_loc, p_loc = [], []
        for i in range(CHUNK_LEN):
            grp = slice(i * V7X_SUBLANES, (i + 1) * V7X_SUBLANES)
            h_loc.append(u[grp, :] if i == 0 else a[grp, :] * h_loc[-1] + u[grp, :])
            p_loc.append(a[grp, :] if i == 0 else a[grp, :] * p_loc[-1])
        h_in = jnp.where(sub8 == 0, pltpu.roll(hstate[...], 1, axis=0), 0.0)
        for c in range(1, N_CHUNKS):
            ends = h_loc[-1] + p_loc[-1] * h_in
            h_in = jnp.where(sub8 == c, pltpu.roll(ends, 1, axis=0), h_in)
        hstate[...] = h_loc[-1] + p_loc[-1] * h_in

        for i in range(CHUNK_LEN):
            grp = slice(i * V7X_SUBLANES, (i + 1) * V7X_SUBLANES)
            out = (h_loc[i] + p_loc[i] * h_in) * _gelu_tanh(gr[grp, :])
            for si, sl in enumerate(slabs):
                g_scr[si, grp, :] = out[:, sl]
        for c in range(N_CHUNKS):
            tok = slice(sub * SUB_ROWS + c * CHUNK_LEN, sub * SUB_ROWS + (c + 1) * CHUNK_LEN)
            gated_ref[tok, :] = jnp.concatenate(
                [g_scr[si, pl.ds(c, CHUNK_LEN, stride=N_CHUNKS), :] for si in range(len(slabs))],
                axis=1).astype(gated_ref.dtype)


def _grouped_spec(d, rows, width, steps_per_seq):
    return pl.BlockSpec((None, d, rows // d, width),
                        lambda i: (i // steps_per_seq, 0, i % steps_per_seq, 0))


def _in_proj_lru(x2, pos2, g, freq, w_in, conv_w, conv_b, wrg, brg, wig, big, lam, B, S, tm):
    T = B * S
    nt = S // tm
    n_sub = tm // SUB_ROWS
    row = lambda w: pl.BlockSpec((tm, w), lambda i: (i, 0))
    qkv_specs = [_grouped_spec(d, tm, GROUP_WIDTH, nt) for d in DILATIONS] * 3
    qkv_shapes = [jax.ShapeDtypeStruct((B, d, S // d, GROUP_WIDTH), BF16) for d in DILATIONS] * 3
    flat = lambda w: jax.ShapeDtypeStruct((T, w), BF16)
    vec = _const_spec((1, D_RNN))
    gate_w = _const_spec((N_LRU_TILES, V7X_MXU_DIM, V7X_MXU_DIM))
    return pl.pallas_call(
        functools.partial(_in_proj_lru_kernel, tm=tm, steps_per_seq=nt),
        grid=(T // tm,),
        in_specs=[row(D_MODEL), row(1), _const_spec((1, D_MODEL)), _const_spec((1, HEAD_DIM)),
                  _const_spec((D_MODEL, IN_WIDTH)),
                  _const_spec((CONV_WIDTH, D_RNN)), vec, gate_w, vec, gate_w, vec, vec],
        out_specs=[row(D_RNN)] + qkv_specs + [row(2 * D_MODEL)],
        out_shape=[flat(D_RNN)] + qkv_shapes + [flat(2 * D_MODEL)],
        scratch_shapes=[pltpu.VMEM((n_sub, D_MODEL // V7X_LANES, SUB_ROWS, V7X_LANES), F32),
                        pltpu.VMEM((n_sub, 3, SUB_ROWS, V7X_LANES), F32),
                        pltpu.VMEM((D_MODEL // V7X_LANES, SUB_ROWS, V7X_LANES), F32),
                        pltpu.VMEM((CONV_TAIL, D_RNN), F32),
                        pltpu.VMEM((V7X_SUBLANES, D_RNN), F32),
                        pltpu.VMEM((D_RNN // V7X_LANES, SUB_ROWS, V7X_LANES), F32)],
        compiler_params=pltpu.CompilerParams(
            dimension_semantics=("arbitrary",), vmem_limit_bytes=VMEM_LIMIT_BYTES),
        name="in_proj_lru",
    )(x2, pos2, g, freq, w_in, conv_w, conv_b, wrg, brg, wig, big, lam)


def _attn_kernel(q_ref, k_ref, v_ref, o_ref, lse_ref, *, n_res, n_blk):
    qi = lax.broadcasted_iota(jnp.int32, (Q_BLOCK, 2 * Q_BLOCK), 0)
    kj = lax.broadcasted_iota(jnp.int32, (Q_BLOCK, 2 * Q_BLOCK), 1)
    band2 = (kj >= qi) & (kj <= qi + Q_BLOCK)
    band1 = (lax.broadcasted_iota(jnp.int32, (Q_BLOCK, Q_BLOCK), 1)
             <= lax.broadcasted_iota(jnp.int32, (Q_BLOCK, Q_BLOCK), 0))
    lane = lax.broadcasted_iota(jnp.int32, (Q_BLOCK, HEAD_DIM), 1)
    seg = HEAD_DIM // HEADS_PER_GROUP

    def rows_of(n):
        q_rows = slice(n * Q_BLOCK, (n + 1) * Q_BLOCK)
        kv_rows = slice(max(n - 1, 0) * Q_BLOCK, (n + 1) * Q_BLOCK)
        return q_rows, kv_rows

    def scores(r, n):
        q_rows, kv_rows = rows_of(n)
        band = band1 if n == 0 else band2
        out = []
        for cols in _slabs(GROUP_WIDTH):
            s = lax.dot_general(q_ref[r, q_rows, cols], k_ref[r, kv_rows, cols],
                                (((1,), (1,)), ((), ())), preferred_element_type=F32)
            out.append(jnp.where(band, s, NEG))
        return out

    def finish(r, n, s_list):
        q_rows, kv_rows = rows_of(n)
        ps, dens = [], []
        lse_tile = jnp.zeros((Q_BLOCK, HEAD_DIM), F32)
        for hd, s in enumerate(s_list):
            m = jnp.max(s, axis=-1, keepdims=True)
            p = jnp.exp(s - m)
            den = jnp.sum(p, axis=-1, keepdims=True)
            ps.append(p.astype(BF16))
            dens.append(den)
            lse_tile = jnp.where(lane // seg == hd, m + jnp.log(den), lse_tile)
        lse_ref[r, q_rows, :] = lse_tile
        for hd, cols in enumerate(_slabs(GROUP_WIDTH)):
            o = jnp.dot(ps[hd], v_ref[r, kv_rows, cols], preferred_element_type=F32) / dens[hd]
            o_ref[r, q_rows, cols] = o.astype(o_ref.dtype)

    blocks = [(r, n) for r in range(n_res) for n in range(n_blk)]
    pending = None
    for blk in blocks:
        s_list = scores(*blk)
        if pending is not None:
            finish(*pending)
        pending = (*blk, s_list)
    finish(*pending)


def _attention_group(q, k, v):
    B, d, L, _ = q.shape
    qkv_spec = pl.BlockSpec((None, d, L, GROUP_WIDTH), lambda b: (b, 0, 0, 0))
    lse_spec = pl.BlockSpec((None, d, L, HEAD_DIM), lambda b: (b, 0, 0, 0))
    return pl.pallas_call(
        functools.partial(_attn_kernel, n_res=d, n_blk=L // Q_BLOCK),
        grid=(B,),
        in_specs=[qkv_spec, qkv_spec, qkv_spec],
        out_specs=[qkv_spec, lse_spec],
        out_shape=[jax.ShapeDtypeStruct((B, d, L, GROUP_WIDTH), BF16),
                   jax.ShapeDtypeStruct((B, d, L, HEAD_DIM), F32)],
        compiler_params=pltpu.CompilerParams(
            dimension_semantics=("arbitrary",), vmem_limit_bytes=VMEM_LIMIT_BYTES),
        name=f"attention_d{d}",
    )(q, k, v)


FF_CHUNK = 2 * V7X_MXU_DIM


def _merge_ffn_kernel(x_ref, gated_ref, o1_ref, o2_ref, o3_ref, l1_ref, l2_ref, l3_ref, gates_ref,
                      wl_ref, wa_ref, wo_ref, gpost_ref, gpre_ref, wg_ref, wu_ref, wd_ref, gffn_ref,
                      out_ref, act_ref, o_scr, l_scr, x1_scr, *, tm):
    seg = HEAD_DIM // HEADS_PER_GROUP

    def mix_phase(sub):
        rows = slice(sub * SUB_ROWS, (sub + 1) * SUB_ROWS)
        y_lru = jnp.dot(gated_ref[rows, :], wl_ref[...], preferred_element_type=F32)

        for g, (d, o_ref, l_ref) in enumerate(zip(DILATIONS, (o1_ref, o2_ref, o3_ref), (l1_ref, l2_ref, l3_ref))):
            n = SUB_ROWS // d
            src = slice(sub * n, (sub + 1) * n)
            for r in range(d):
                dst = pl.ds(r, n, stride=d) if d > 1 else pl.ds(0, n)
                l_scr[sub, g, dst, :] = l_ref[r, src, :]
                for c, sl in enumerate(_slabs(GROUP_WIDTH)):
                    o_scr[sub, g, c, dst, :] = o_ref[r, src, sl].astype(F32)

        ls = [l_scr[sub, g] for g in range(N_GROUPS)]
        mx = jnp.maximum(jnp.maximum(ls[0], ls[1]), ls[2])
        es = [jnp.exp(l - mx) for l in ls]
        inv = 1.0 / (es[0] + es[1] + es[2])
        heads = []
        for hd in range(HEADS_PER_GROUP):
            acc = None
            for g in range(N_GROUPS):
                w = (es[g] * inv)[:, hd * seg:hd * seg + 1]
                term = w * o_scr[sub, g, hd]
                acc = term if acc is None else acc + term
            heads.append(acc)
        o = jnp.concatenate(heads, axis=1).astype(BF16)
        y_attn = jnp.dot(o, wa_ref[...], preferred_element_type=F32)

        g_lru = _sigmoid(gates_ref[rows, :D_MODEL].astype(F32))
        g_attn = _sigmoid(gates_ref[rows, D_MODEL:].astype(F32))
        merged = (g_lru * y_lru + g_attn * y_attn).astype(BF16)
        mix = jnp.dot(merged, wo_ref[...], preferred_element_type=F32)
        x1 = x_ref[rows, :] + _rms_norm(mix, gpost_ref[...])
        x1_scr[sub] = x1
        return _rms_norm(x1, gpre_ref[...]).astype(BF16)

    def ffn_phase(sub, h):
        rows = slice(sub * SUB_ROWS, (sub + 1) * SUB_ROWS)
        for c in range(0, D_FF, FF_CHUNK):
            w = min(FF_CHUNK, D_FF - c)
            gate = jnp.dot(h, wg_ref[:, c:c + w], preferred_element_type=F32)
            up = jnp.dot(h, wu_ref[:, c:c + w], preferred_element_type=F32)
            act_ref[sub, :, c:c + w] = (gate * _sigmoid(gate) * up).astype(BF16)
        f = jnp.dot(act_ref[sub], wd_ref[...], preferred_element_type=F32)
        out_ref[rows, :] = x1_scr[sub] + _rms_norm(f, gffn_ref[...])

    n_sub = tm // SUB_ROWS
    hs = [mix_phase(sub) for sub in range(n_sub)]
    for sub in range(n_sub):
        ffn_phase(sub, hs[sub])


def _merge_ffn(x2, gated, os, ls, gates, wl, wa, wo, gpost, gpre, wg, wu, wd, gffn, S, tm):
    T = x2.shape[0]
    nt = S // tm
    row = lambda w: pl.BlockSpec((tm, w), lambda i: (i, 0))
    vec = _const_spec((1, D_MODEL))
    return pl.pallas_call(
        functools.partial(_merge_ffn_kernel, tm=tm),
        grid=(T // tm,),
        in_specs=[row(D_MODEL), row(D_RNN)]
                 + [_grouped_spec(d, tm, GROUP_WIDTH, nt) for d in DILATIONS]
                 + [_grouped_spec(d, tm, HEAD_DIM, nt) for d in DILATIONS]
                 + [row(2 * D_MODEL),
                    _const_spec((D_RNN, D_MODEL)), _const_spec((GROUP_WIDTH, D_MODEL)),
                    _const_spec((D_MODEL, D_MODEL)), vec, vec,
                    _const_spec((D_MODEL, D_FF)), _const_spec((D_MODEL, D_FF)),
                    _const_spec((D_FF, D_MODEL)), vec],
        out_specs=row(D_MODEL),
        out_shape=jax.ShapeDtypeStruct((T, D_MODEL), F32),
        scratch_shapes=[pltpu.VMEM((tm // SUB_ROWS, SUB_ROWS, D_FF), BF16),
                        pltpu.VMEM((tm // SUB_ROWS, N_GROUPS, HEADS_PER_GROUP, SUB_ROWS, HEAD_DIM), F32),
                        pltpu.VMEM((tm // SUB_ROWS, N_GROUPS, SUB_ROWS, HEAD_DIM), F32),
                        pltpu.VMEM((tm // SUB_ROWS, SUB_ROWS, D_MODEL), F32)],
        compiler_params=pltpu.CompilerParams(
            dimension_semantics=("arbitrary",), vmem_limit_bytes=VMEM_LIMIT_BYTES),
        name="merge_ffn",
    )(x2, gated, *os, *ls, gates, wl, wa, wo, gpost, gpre, wg, wu, wd, gffn)


def _pack_lru_gate(w):
    w4 = w.reshape(N_LRU_TILES, LRU_PACK, LRU_BLOCK, LRU_BLOCK)
    eye = jnp.eye(LRU_PACK, dtype=w.dtype)
    packed = jnp.einsum('jacd,ab->jacbd', w4, eye)
    return packed.reshape(N_LRU_TILES, V7X_MXU_DIM, V7X_MXU_DIM).astype(BF16)


def kernel(x, positions, pre_mix_norm, w_in, conv_w, conv_b, w_rg, b_rg, w_ig, b_ig, lru_lambda,
           w_lru_proj, w_attn_proj, w_out, post_mix_norm, pre_ffn_norm, w_ffn_gate, w_ffn_up,
           w_ffn_down, post_ffn_norm):
    B, S, D = x.shape
    assert D == D_MODEL and S % TM == 0 and TM % SUB_ROWS == 0
    assert SUB_ROWS % (DILATIONS[-1] * 2 * V7X_SUBLANES) == 0
    assert pre_mix_norm.shape[0] == 1, "single-layer block"
    T = B * S

    inv_freq = ROPE_THETA ** (-jnp.arange(0, ROPE_DIM, 2, dtype=F32) / ROPE_DIM)
    freq = jnp.concatenate([inv_freq, inv_freq, jnp.zeros((HEAD_DIM - ROPE_DIM,), F32)])[None, :]

    x2 = x.reshape(T, D)
    pos2 = positions.reshape(T, 1)
    row = lambda p: p[0][None, :]

    gated, q1, q2, q3, k1, k2, k3, v1, v2, v3, gates = _in_proj_lru(
        x2, pos2, row(pre_mix_norm), freq, w_in[0].astype(BF16), conv_w[0], row(conv_b),
        _pack_lru_gate(w_rg[0]), row(b_rg), _pack_lru_gate(w_ig[0]), row(b_ig), row(lru_lambda),
        B, S, TM)

    os, ls = [], []
    for q, k, v in ((q1, k1, v1), (q2, k2, v2), (q3, k3, v3)):
        o, l = _attention_group(q, k, v)
        os.append(o)
        ls.append(l)

    out = _merge_ffn(x2, gated, os, ls, gates,
                     w_lru_proj[0].astype(BF16), w_attn_proj[0].astype(BF16), w_out[0].astype(BF16),
                     row(post_mix_norm), row(pre_ffn_norm), w_ffn_gate[0].astype(BF16),
                     w_ffn_up[0].astype(BF16), w_ffn_down[0].astype(BF16), row(post_ffn_norm), S, TM)
    return out.reshape(B, S, D)
```

```python
import functools
import math

import jax
import jax.numpy as jnp
from jax import lax
from jax.experimental import pallas as pl
from jax.experimental.pallas import tpu as pltpu

D_MODEL = 1024
D_RNN = 1024
N_LRU_BLOCKS = 16
LRU_BLOCK = D_RNN // N_LRU_BLOCKS
CONV_WIDTH = 4
LRU_C = 8.0
ATTN_GROUPS = ((128, 1), (512, 4), (2048, 16))
DILATIONS = tuple(d for _, d in ATTN_GROUPS)
N_GROUPS = len(ATTN_GROUPS)
HEADS_PER_GROUP = 4
HEAD_DIM = 128
GROUP_WIDTH = HEADS_PER_GROUP * HEAD_DIM
ATTN_WIDTH = N_GROUPS * GROUP_WIDTH
ROPE_DIM = HEAD_DIM // 4
ROPE_THETA = 500000.0
Q_BLOCK = 128
D_FF = ((8 * D_MODEL // 3 + 255) // 256) * 256
IN_WIDTH = 2 * D_RNN + 3 * ATTN_WIDTH + 2 * D_MODEL
EPS = 1e-6
NEG = -1e30

V7X_LANES = 128
V7X_SUBLANES = 8
V7X_MXU_DIM = 256
V7X_VMEM_BYTES = 64 * 1024 * 1024
VMEM_LIMIT_BYTES = V7X_VMEM_BYTES - 8 * 1024 * 1024

BF16 = jnp.bfloat16
F32 = jnp.float32

LRU_PACK = V7X_MXU_DIM // LRU_BLOCK
N_LRU_TILES = N_LRU_BLOCKS // LRU_PACK

TM = 512
SUB_ROWS = 256
N_CHUNKS = V7X_SUBLANES
CHUNK_LEN = SUB_ROWS // N_CHUNKS
CONV_TAIL = (CONV_WIDTH - 1) * V7X_SUBLANES

assert all(w // d == Q_BLOCK for w, d in ATTN_GROUPS), "band logic assumes window == dilation * Q_BLOCK"
assert CONV_WIDTH - 1 <= CHUNK_LEN


def _rms_norm(x, g):
    return x * lax.rsqrt(jnp.mean(x * x, axis=-1, keepdims=True) + EPS) * g


def _sigmoid(x):
    return 0.5 * jnp.tanh(0.5 * x) + 0.5


def _gelu_tanh(x):
    c = math.sqrt(2.0 / math.pi)
    return x * (0.5 * jnp.tanh(x * (c + (c * 0.044715) * (x * x))) + 0.5)


def _const_spec(shape):
    nd = len(shape)
    return pl.BlockSpec(shape, lambda *_: (0,) * nd, pipeline_mode=pl.Buffered(1))


def _slabs(width):
    return [slice(c * V7X_LANES, (c + 1) * V7X_LANES) for c in range(width // V7X_LANES)]


def _residue_major(slab_ref, d, n_rows):
    if d == 1:
        return jnp.concatenate([slab_ref[c] for c in range(slab_ref.shape[0])], axis=1)
    n = n_rows // d
    return jnp.concatenate(
        [jnp.concatenate([slab_ref[c, pl.ds(r, n, stride=d), :] for c in range(slab_ref.shape[0])], axis=1)
         for r in range(d)], axis=0)


IN_CHUNK = GROUP_WIDTH


def _in_proj_lru_kernel(x_ref, pos_ref, g_ref, freq_ref, w_ref,
                        cw_ref, cb_ref, wrg_ref, brg_ref, wig_ref, big_ref, lam_ref,
                        gated_ref, q1_ref, q2_ref, q3_ref, k1_ref, k2_ref, k3_ref,
                        v1_ref, v2_ref, v3_ref, gates_ref,
                        h_scr, rope_scr, hl_scr, lhs_scr, gate_scr, xr_halo, hstate, g_scr, *, tm, steps_per_seq):
    half = ROPE_DIM // 2
    scale = HEAD_DIM ** -0.5
    q_base = 2 * D_RNN // IN_CHUNK
    gates_base = q_base + 3 * N_GROUPS
    seq_pos = (pl.program_id(0) % steps_per_seq) * tm
    sub8 = lax.broadcasted_iota(jnp.int32, (V7X_SUBLANES, D_RNN), 0)
    slabs = _slabs(D_MODEL)

    def rope(y, tabs, mul):
        c, s_lo, s_hi = (tabs[:, sl] if mul == 1.0 else tabs[:, sl] * mul for sl in _slabs(3 * HEAD_DIM))
        outs = []
        for sl in _slabs(GROUP_WIDTH):
            t = y[:, sl]
            up = pltpu.roll(t, HEAD_DIM - half, axis=1)
            dn = pltpu.roll(t, half, axis=1)
            outs.append(t * c + up * s_lo + dn * s_hi)
        return jnp.concatenate(outs, axis=1)

    def chunk(lhs_ref, ci):
        return jnp.dot(lhs_ref[...], w_ref[:, ci * IN_CHUNK:(ci + 1) * IN_CHUNK], preferred_element_type=F32)

    def wide(lhs, first_chunk):
        return jnp.concatenate([chunk(lhs, first_chunk + j) for j in range(D_RNN // IN_CHUNK)], axis=1)

    nl = -lam_ref[...]
    softplus = jnp.maximum(nl, 0.0) + jnp.log1p(jnp.exp(-jnp.abs(nl)))
    log_a_coef = (-LRU_C) * softplus

    @pl.when(seq_pos == 0)
    def _():
        xr_halo[...] = jnp.zeros(xr_halo.shape, F32)
        hstate[...] = jnp.zeros(hstate.shape, F32)

    n_sub = tm // SUB_ROWS
    dyn_zero = jnp.minimum(pl.program_id(0), 0)

    def emit(sub, ref, y, d):
        n = SUB_ROWS // d
        for r in range(d):
            ref[r, sub * n:(sub + 1) * n, :] = y[r * n:(r + 1) * n, :].astype(ref.dtype)

    def qkv(sub, g):
        d, lhs_g = DILATIONS[g], lhs_scr.at[sub, 1 + g]
        tabs = _residue_major(rope_scr.at[sub], d, SUB_ROWS)
        emit(sub, (q1_ref, q2_ref, q3_ref)[g], rope(chunk(lhs_g, q_base + g), tabs, scale), d)
        emit(sub, (k1_ref, k2_ref, k3_ref)[g], rope(chunk(lhs_g, q_base + N_GROUPS + g), tabs, 1.0), d)
        emit(sub, (v1_ref, v2_ref, v3_ref)[g], chunk(lhs_g, q_base + 2 * N_GROUPS + g), d)

    for sub in range(n_sub):
        rows = slice(sub * SUB_ROWS, (sub + 1) * SUB_ROWS)
        h_sub, rope_sub = h_scr.at[sub], rope_scr.at[sub]
        h = _rms_norm(x_ref[rows, :], g_ref[...])
        for c, sl in enumerate(slabs):
            h_sub[c] = h[:, sl]
        for c in range(N_CHUNKS):
            for si, sl in enumerate(slabs):
                hl_scr[si, pl.ds(c, CHUNK_LEN, stride=N_CHUNKS), :] = h[c * CHUNK_LEN:(c + 1) * CHUNK_LEN, sl]
        lhs = lhs_scr.at[sub]
        lhs[0] = jnp.concatenate([hl_scr[si] for si in range(len(slabs))], axis=1).astype(BF16)
        for g, d in enumerate(DILATIONS):
            lhs[1 + g] = _residue_major(h_sub, d, SUB_ROWS).astype(BF16)

        ang = freq_ref[...] * pos_ref[:, rows].astype(F32)
        cos_t, sin_t = jnp.cos(ang), jnp.sin(ang)
        rest = HEAD_DIM - ROPE_DIM
        zeros_t = lambda n: jnp.zeros((n, SUB_ROWS), F32)
        rope_sub[0] = jnp.concatenate([cos_t, jnp.ones((rest, SUB_ROWS), F32)], axis=0).T
        rope_sub[1] = jnp.concatenate([-sin_t[:half], zeros_t(rest + half)], axis=0).T
        rope_sub[2] = jnp.concatenate([zeros_t(half), sin_t[half:], zeros_t(rest)], axis=0).T

        xr = wide(lhs.at[0], 0)
        gr = wide(lhs.at[0], D_RNN // IN_CHUNK)
        qkv(sub, 0)

        prev = xr_halo[...]
        wraps = []
        for j in range(CONV_WIDTH - 1):
            grp = slice(j * V7X_SUBLANES, (j + 1) * V7X_SUBLANES)
            cur = xr[SUB_ROWS - CONV_TAIL + j * V7X_SUBLANES:SUB_ROWS - CONV_TAIL + (j + 1) * V7X_SUBLANES, :]
            wraps.append(jnp.where(sub8 == 0, pltpu.roll(prev[grp, :], 1, axis=0), pltpu.roll(cur, 1, axis=0)))
        xr_halo[...] = xr[SUB_ROWS - CONV_TAIL:, :]
        xe = jnp.concatenate(wraps + [xr], axis=0)
        y = cb_ref[...]
        for s in range(CONV_WIDTH):
            lo = CONV_TAIL - s * V7X_SUBLANES
            y = y + xe[lo:lo + SUB_ROWS, :] * cw_ref[CONV_WIDTH - 1 - s:CONV_WIDTH - s, :]
        yb = y.astype(BF16)

        for j in range(2 * D_MODEL // IN_CHUNK):
            cols = slice(j * IN_CHUNK, (j + 1) * IN_CHUNK)
            gates_ref[rows, cols] = chunk(lhs.at[1], gates_base + j).astype(gates_ref.dtype)
        qkv(sub, 1)
        pre = gate_scr.at[dyn_zero]
        for j in range(N_LRU_TILES):
            sl = slice(j * V7X_MXU_DIM, (j + 1) * V7X_MXU_DIM)
            pre[0, :, sl] = jnp.dot(yb[:, sl], wrg_ref[j], preferred_element_type=F32)
            pre[1, :, sl] = jnp.dot(yb[:, sl], wig_ref[j], preferred_element_type=F32)
        for g in range(2, N_GROUPS):
            qkv(sub, g)

        r = _sigmoid(pre[0] + brg_ref[...])
        ig = _sigmoid(pre[1] + big_ref[...])
        a = jnp.exp(log_a_coef * r)
        mult = jnp.sqrt(1.0 - a * a)
        if sub == 0:
            row = seq_pos + lax.broadcasted_iota(jnp.int32, (SUB_ROWS, 1), 0)
            mult = jnp.where(row == 0, 1.0, mult)
        u = mult * (ig * y)

        h_loc, p_loc = [], []
        for i in range(CHUNK_LEN):
            grp = slice(i * V7X_SUBLANES, (i + 1) * V7X_SUBLANES)
            h_loc.append(u[grp, :] if i == 0 else a[grp, :] * h_loc[-1] + u[grp, :])
            p_loc.append(a[grp, :] if i == 0 else a[grp, :] * p_loc[-1])
        h_in = jnp.where(sub8 == 0, pltpu.roll(hstate[...], 1, axis=0), 0.0)
        for c in range(1, N_CHUNKS):
            ends = h_loc[-1] + p_loc[-1] * h_in
            h_in = jnp.where(sub8 == c, pltpu.roll(ends, 1, axis=0), h_in)
        hstate[...] = h_loc[-1] + p_loc[-1] * h_in

        for i in range(CHUNK_LEN):
            grp = slice(i * V7X_SUBLANES, (i + 1) * V7X_SUBLANES)
            out = (h_loc[i] + p_loc[i] * h_in) * _gelu_tanh(gr[grp, :])
            for si, sl in enumerate(slabs):
                g_scr[si, grp, :] = out[:, sl]
        for c in range(N_CHUNKS):
            tok = slice(sub * SUB_ROWS + c * CHUNK_LEN, sub * SUB_ROWS + (c + 1) * CHUNK_LEN)
            gated_ref[tok, :] = jnp.concatenate(
                [g_scr[si, pl.ds(c, CHUNK_LEN, stride=N_CHUNKS), :] for si in range(len(slabs))],
                axis=1).astype(gated_ref.dtype)


def _grouped_spec(d, rows, width, steps_per_seq):
    return pl.BlockSpec((None, d, rows // d, width),
                        lambda i: (i // steps_per_seq, 0, i % steps_per_seq, 0))


def _in_proj_lru(x2, pos2, g, freq, w_in, conv_w, conv_b, wrg, brg, wig, big, lam, B, S, tm):
    T = B * S
    nt = S // tm
    n_sub = tm // SUB_ROWS
    row = lambda w: pl.BlockSpec((tm, w), lambda i: (i, 0))
    qkv_specs = [_grouped_spec(d, tm, GROUP_WIDTH, nt) for d in DILATIONS] * 3
    qkv_shapes = [jax.ShapeDtypeStruct((B, d, S // d, GROUP_WIDTH), BF16) for d in DILATIONS] * 3
    flat = lambda w: jax.ShapeDtypeStruct((T, w), BF16)
    vec = _const_spec((1, D_RNN))
    gate_w = _const_spec((N_LRU_TILES, V7X_MXU_DIM, V7X_MXU_DIM))
    return pl.pallas_call(
        functools.partial(_in_proj_lru_kernel, tm=tm, steps_per_seq=nt),
        grid=(T // tm,),
        in_specs=[row(D_MODEL), pl.BlockSpec((None, 1, tm), lambda i: (i, 0, 0)),
                  _const_spec((1, D_MODEL)), _const_spec((ROPE_DIM, 1)),
                  _const_spec((D_MODEL, IN_WIDTH)),
                  _const_spec((CONV_WIDTH, D_RNN)), vec, gate_w, vec, gate_w, vec, vec],
        out_specs=[row(D_RNN)] + qkv_specs + [row(2 * D_MODEL)],
        out_shape=[flat(D_RNN)] + qkv_shapes + [flat(2 * D_MODEL)],
        scratch_shapes=[pltpu.VMEM((n_sub, D_MODEL // V7X_LANES, SUB_ROWS, V7X_LANES), F32),
                        pltpu.VMEM((n_sub, 3, SUB_ROWS, V7X_LANES), F32),
                        pltpu.VMEM((D_MODEL // V7X_LANES, SUB_ROWS, V7X_LANES), F32),
                        pltpu.VMEM((n_sub, 1 + N_GROUPS, SUB_ROWS, D_MODEL), BF16),
                        pltpu.VMEM((1, 2, SUB_ROWS, D_RNN), F32),
                        pltpu.VMEM((CONV_TAIL, D_RNN), F32),
                        pltpu.VMEM((V7X_SUBLANES, D_RNN), F32),
                        pltpu.VMEM((D_RNN // V7X_LANES, SUB_ROWS, V7X_LANES), F32)],
        compiler_params=pltpu.CompilerParams(
            dimension_semantics=("arbitrary",), vmem_limit_bytes=VMEM_LIMIT_BYTES),
        name="in_proj_lru",
    )(x2, pos2, g, freq, w_in, conv_w, conv_b, wrg, brg, wig, big, lam)


def _attn_kernel(q_ref, k_ref, v_ref, o_ref, lse_ref, *, n_res, n_blk):
    qi = lax.broadcasted_iota(jnp.int32, (Q_BLOCK, 2 * Q_BLOCK), 0)
    kj = lax.broadcasted_iota(jnp.int32, (Q_BLOCK, 2 * Q_BLOCK), 1)
    band2 = (kj >= qi) & (kj <= qi + Q_BLOCK)
    band1 = (lax.broadcasted_iota(jnp.int32, (Q_BLOCK, Q_BLOCK), 1)
             <= lax.broadcasted_iota(jnp.int32, (Q_BLOCK, Q_BLOCK), 0))
    lane = lax.broadcasted_iota(jnp.int32, (Q_BLOCK, HEAD_DIM), 1)
    seg = HEAD_DIM // HEADS_PER_GROUP

    def rows_of(n):
        q_rows = slice(n * Q_BLOCK, (n + 1) * Q_BLOCK)
        kv_rows = slice(max(n - 1, 0) * Q_BLOCK, (n + 1) * Q_BLOCK)
        return q_rows, kv_rows

    def scores(r, n):
        q_rows, kv_rows = rows_of(n)
        band = band1 if n == 0 else band2
        out = []
        for cols in _slabs(GROUP_WIDTH):
            s = lax.dot_general(q_ref[r, q_rows, cols], k_ref[r, kv_rows, cols],
                                (((1,), (1,)), ((), ())), preferred_element_type=F32)
            out.append(jnp.where(band, s, NEG))
        return out

    def finish(r, n, s_list):
        q_rows, kv_rows = rows_of(n)
        ps, dens = [], []
        lse_tile = jnp.zeros((Q_BLOCK, HEAD_DIM), F32)
        for hd, s in enumerate(s_list):
            m = jnp.max(s, axis=-1, keepdims=True)
            p = jnp.exp(s - m)
            den = jnp.sum(p, axis=-1, keepdims=True)
            ps.append(p.astype(BF16))
            dens.append(den)
            lse_tile = jnp.where(lane // seg == hd, m + jnp.log(den), lse_tile)
        lse_ref[r, q_rows, :] = lse_tile
        for hd, cols in enumerate(_slabs(GROUP_WIDTH)):
            o = jnp.dot(ps[hd], v_ref[r, kv_rows, cols], preferred_element_type=F32) / dens[hd]
            o_ref[r, q_rows, cols] = o.astype(o_ref.dtype)

    blocks = [(r, n) for r in range(n_res) for n in range(n_blk)]
    pending = None
    for blk in blocks:
        s_list = scores(*blk)
        if pending is not None:
            finish(*pending)
        pending = (*blk, s_list)
    finish(*pending)


def _attention_group(q, k, v):
    B, d, L, _ = q.shape
    qkv_spec = pl.BlockSpec((None, d, L, GROUP_WIDTH), lambda b: (b, 0, 0, 0))
    lse_spec = pl.BlockSpec((None, d, L, HEAD_DIM), lambda b: (b, 0, 0, 0))
    return pl.pallas_call(
        functools.partial(_attn_kernel, n_res=d, n_blk=L // Q_BLOCK),
        grid=(B,),
        in_specs=[qkv_spec, qkv_spec, qkv_spec],
        out_specs=[qkv_spec, lse_spec],
        out_shape=[jax.ShapeDtypeStruct((B, d, L, GROUP_WIDTH), BF16),
                   jax.ShapeDtypeStruct((B, d, L, HEAD_DIM), F32)],
        compiler_params=pltpu.CompilerParams(
            dimension_semantics=("arbitrary",), vmem_limit_bytes=VMEM_LIMIT_BYTES),
        name=f"attention_d{d}",
    )(q, k, v)


FF_CHUNK = 2 * V7X_MXU_DIM


def _merge_ffn_kernel(x_ref, gated_ref, o1_ref, o2_ref, o3_ref, l1_ref, l2_ref, l3_ref, gates_ref,
                      wl_ref, wa_ref, wo_ref, gpost_ref, gpre_ref, wg_ref, wu_ref, wd_ref, gffn_ref,
                      out_ref, act_ref, o_scr, l_scr, x1_scr, *, tm):
    seg = HEAD_DIM // HEADS_PER_GROUP

    def mix_phase(sub):
        rows = slice(sub * SUB_ROWS, (sub + 1) * SUB_ROWS)
        y_lru = jnp.dot(gated_ref[rows, :], wl_ref[...], preferred_element_type=F32)

        for g, (d, o_ref, l_ref) in enumerate(zip(DILATIONS, (o1_ref, o2_ref, o3_ref), (l1_ref, l2_ref, l3_ref))):
            n = SUB_ROWS // d
            src = slice(sub * n, (sub + 1) * n)
            for r in range(d):
                dst = pl.ds(r, n, stride=d) if d > 1 else pl.ds(0, n)
                l_scr[sub, g, dst, :] = l_ref[r, src, :]
                for c, sl in enumerate(_slabs(GROUP_WIDTH)):
                    o_scr[sub, g, c, dst, :] = o_ref[r, src, sl].astype(F32)

        ls = [l_scr[sub, g] for g in range(N_GROUPS)]
        mx = jnp.maximum(jnp.maximum(ls[0], ls[1]), ls[2])
        es = [jnp.exp(l - mx) for l in ls]
        inv = 1.0 / (es[0] + es[1] + es[2])
        heads = []
        for hd in range(HEADS_PER_GROUP):
            acc = None
            for g in range(N_GROUPS):
                w = (es[g] * inv)[:, hd * seg:hd * seg + 1]
                term = w * o_scr[sub, g, hd]
                acc = term if acc is None else acc + term
            heads.append(acc)
        o = jnp.concatenate(heads, axis=1).astype(BF16)
        y_attn = jnp.dot(o, wa_ref[...], preferred_element_type=F32)

        g_lru = _sigmoid(gates_ref[rows, :D_MODEL].astype(F32))
        g_attn = _sigmoid(gates_ref[rows, D_MODEL:].astype(F32))
        merged = (g_lru * y_lru + g_attn * y_attn).astype(BF16)
        mix = jnp.dot(merged, wo_ref[...], preferred_element_type=F32)
        x1 = x_ref[rows, :] + _rms_norm(mix, gpost_ref[...])
        x1_scr[sub] = x1
        return _rms_norm(x1, gpre_ref[...]).astype(BF16)

    def ffn_phase(sub, h):
        rows = slice(sub * SUB_ROWS, (sub + 1) * SUB_ROWS)
        for c in range(0, D_FF, FF_CHUNK):
            w = min(FF_CHUNK, D_FF - c)
            gate = jnp.dot(h, wg_ref[:, c:c + w], preferred_element_type=F32)
            up = jnp.dot(h, wu_ref[:, c:c + w], preferred_element_type=F32)
            act_ref[sub, :, c:c + w] = (gate * _sigmoid(gate) * up).astype(BF16)
        f = jnp.dot(act_ref[sub], wd_ref[...], preferred_element_type=F32)
        out_ref[rows, :] = x1_scr[sub] + _rms_norm(f, gffn_ref[...])

    n_sub = tm // SUB_ROWS
    hs = [mix_phase(sub) for sub in range(n_sub)]
    for sub in range(n_sub):
        ffn_phase(sub, hs[sub])


def _merge_ffn(x2, gated, os, ls, gates, wl, wa, wo, gpost, gpre, wg, wu, wd, gffn, S, tm):
    T = x2.shape[0]
    nt = S // tm
    row = lambda w: pl.BlockSpec((tm, w), lambda i: (i, 0))
    vec = _const_spec((1, D_MODEL))
    return pl.pallas_call(
        functools.partial(_merge_ffn_kernel, tm=tm),
        grid=(T // tm,),
        in_specs=[row(D_MODEL), row(D_RNN)]
                 + [_grouped_spec(d, tm, GROUP_WIDTH, nt) for d in DILATIONS]
                 + [_grouped_spec(d, tm, HEAD_DIM, nt) for d in DILATIONS]
                 + [row(2 * D_MODEL),
                    _const_spec((D_RNN, D_MODEL)), _const_spec((GROUP_WIDTH, D_MODEL)),
                    _const_spec((D_MODEL, D_MODEL)), vec, vec,
                    _const_spec((D_MODEL, D_FF)), _const_spec((D_MODEL, D_FF)),
                    _const_spec((D_FF, D_MODEL)), vec],
        out_specs=row(D_MODEL),
        out_shape=jax.ShapeDtypeStruct((T, D_MODEL), F32),
        scratch_shapes=[pltpu.VMEM((tm // SUB_ROWS, SUB_ROWS, D_FF), BF16),
                        pltpu.VMEM((tm // SUB_ROWS, N_GROUPS, HEADS_PER_GROUP, SUB_ROWS, HEAD_DIM), F32),
                        pltpu.VMEM((tm // SUB_ROWS, N_GROUPS, SUB_ROWS, HEAD_DIM), F32),
                        pltpu.VMEM((tm // SUB_ROWS, SUB_ROWS, D_MODEL), F32)],
        compiler_params=pltpu.CompilerParams(
            dimension_semantics=("arbitrary",), vmem_limit_bytes=VMEM_LIMIT_BYTES),
        name="merge_ffn",
    )(x2, gated, *os, *ls, gates, wl, wa, wo, gpost, gpre, wg, wu, wd, gffn)


def _pack_lru_gate(w):
    w4 = w.reshape(N_LRU_TILES, LRU_PACK, LRU_BLOCK, LRU_BLOCK)
    eye = jnp.eye(LRU_PACK, dtype=w.dtype)
    packed = jnp.einsum('jacd,ab->jacbd', w4, eye)
    return packed.reshape(N_LRU_TILES, V7X_MXU_DIM, V7X_MXU_DIM).astype(BF16)


def kernel(x, positions, pre_mix_norm, w_in, conv_w, conv_b, w_rg, b_rg, w_ig, b_ig, lru_lambda,
           w_lru_proj, w_attn_proj, w_out, post_mix_norm, pre_ffn_norm, w_ffn_gate, w_ffn_up,
           w_ffn_down, post_ffn_norm):
    B, S, D = x.shape
    assert D == D_MODEL and S % TM == 0 and TM % SUB_ROWS == 0
    assert SUB_ROWS % (DILATIONS[-1] * 2 * V7X_SUBLANES) == 0
    assert pre_mix_norm.shape[0] == 1, "single-layer block"
    T = B * S

    inv_freq = ROPE_THETA ** (-jnp.arange(0, ROPE_DIM, 2, dtype=F32) / ROPE_DIM)
    freq = jnp.concatenate([inv_freq, inv_freq])[:, None]

    x2 = x.reshape(T, D)
    pos2 = positions.reshape(T // TM, 1, TM)
    row = lambda p: p[0][None, :]

    gated, q1, q2, q3, k1, k2, k3, v1, v2, v3, gates = _in_proj_lru(
        x2, pos2, row(pre_mix_norm), freq, w_in[0].astype(BF16), conv_w[0], row(conv_b),
        _pack_lru_gate(w_rg[0]), row(b_rg), _pack_lru_gate(w_ig[0]), row(b_ig), row(lru_lambda),
        B, S, TM)

    os, ls = [], []
    for q, k, v in ((q1, k1, v1), (q2, k2, v2), (q3, k3, v3)):
        o, l = _attention_group(q, k, v)
        os.append(o)
        ls.append(l)

    out = _merge_ffn(x2, gated, os, ls, gates,
                     w_lru_proj[0].astype(BF16), w_attn_proj[0].astype(BF16), w_out[0].astype(BF16),
                     row(post_mix_norm), row(pre_ffn_norm), w_ffn_gate[0].astype(BF16),
                     w_ffn_up[0].astype(BF16), w_ffn_down[0].astype(BF16), row(post_ffn_norm), S, TM)
    return out.reshape(B, S, D)
```

```python
import functools
import math

import jax
import jax.numpy as jnp
from jax import lax
from jax.experimental import pallas as pl
from jax.experimental.pallas import tpu as pltpu

D_MODEL = 1024
D_RNN = 1024
N_LRU_BLOCKS = 16
LRU_BLOCK = D_RNN // N_LRU_BLOCKS
CONV_WIDTH = 4
LRU_C = 8.0
ATTN_GROUPS = ((128, 1), (512, 4), (2048, 16))
DILATIONS = tuple(d for _, d in ATTN_GROUPS)
N_GROUPS = len(ATTN_GROUPS)
HEADS_PER_GROUP = 4
HEAD_DIM = 128
GROUP_WIDTH = HEADS_PER_GROUP * HEAD_DIM
ATTN_WIDTH = N_GROUPS * GROUP_WIDTH
ROPE_DIM = HEAD_DIM // 4
ROPE_THETA = 500000.0
Q_BLOCK = 128
D_FF = ((8 * D_MODEL // 3 + 255) // 256) * 256
IN_WIDTH = 2 * D_RNN + 3 * ATTN_WIDTH + 2 * D_MODEL
EPS = 1e-6
NEG = -1e30

V7X_LANES = 128
V7X_SUBLANES = 8
V7X_MXU_DIM = 256
V7X_VMEM_BYTES = 64 * 1024 * 1024
VMEM_LIMIT_BYTES = V7X_VMEM_BYTES - 8 * 1024 * 1024

BF16 = jnp.bfloat16
F32 = jnp.float32

LRU_PACK = V7X_MXU_DIM // LRU_BLOCK
N_LRU_TILES = N_LRU_BLOCKS // LRU_PACK

TM = 512
SUB_ROWS = 256
N_CHUNKS = V7X_SUBLANES
CHUNK_LEN = SUB_ROWS // N_CHUNKS
CONV_TAIL = (CONV_WIDTH - 1) * V7X_SUBLANES

assert all(w // d == Q_BLOCK for w, d in ATTN_GROUPS), "band logic assumes window == dilation * Q_BLOCK"
assert CONV_WIDTH - 1 <= CHUNK_LEN


def _rms_norm(x, g):
    return x * lax.rsqrt(jnp.mean(x * x, axis=-1, keepdims=True) + EPS) * g


def _sigmoid(x):
    return 0.5 * jnp.tanh(0.5 * x) + 0.5


def _gelu_tanh(x):
    c = math.sqrt(2.0 / math.pi)
    return x * (0.5 * jnp.tanh(x * (c + (c * 0.044715) * (x * x))) + 0.5)


def _const_spec(shape):
    nd = len(shape)
    return pl.BlockSpec(shape, lambda *_: (0,) * nd, pipeline_mode=pl.Buffered(1))


def _slabs(width):
    return [slice(c * V7X_LANES, (c + 1) * V7X_LANES) for c in range(width // V7X_LANES)]


def _residue_major(slab_ref, d, n_rows):
    if d == 1:
        return jnp.concatenate([slab_ref[c] for c in range(slab_ref.shape[0])], axis=1)
    n = n_rows // d
    return jnp.concatenate(
        [jnp.concatenate([slab_ref[c, pl.ds(r, n, stride=d), :] for c in range(slab_ref.shape[0])], axis=1)
         for r in range(d)], axis=0)


def _regroup(slab_ref, d_in, d_out, n_rows):
    assert d_out == d_in * d_in
    n_in, n_out = n_rows // d_in, n_rows // d_out
    pieces = []
    for r in range(d_out):
        start = (r % d_in) * n_in + r // d_in
        pieces.append(jnp.concatenate(
            [slab_ref[c, pl.ds(start, n_out, stride=d_in), :] for c in range(slab_ref.shape[0])], axis=1))
    return jnp.concatenate(pieces, axis=0)


IN_CHUNK = GROUP_WIDTH


def _in_proj_lru_kernel(x_ref, pos_ref, g_ref, freq_ref, w_ref,
                        cw_ref, cb_ref, wrg_ref, brg_ref, wig_ref, big_ref, lam_ref,
                        gated_ref, q1_ref, q2_ref, q3_ref, k1_ref, k2_ref, k3_ref,
                        v1_ref, v2_ref, v3_ref, gates_ref,
                        h_scr, rope_scr, hl_scr, h4_scr, lhs_scr, gate_scr, xr_halo, hstate, g_scr, *, tm, steps_per_seq):
    half = ROPE_DIM // 2
    scale = HEAD_DIM ** -0.5
    q_base = 2 * D_RNN // IN_CHUNK
    gates_base = q_base + 3 * N_GROUPS
    seq_pos = (pl.program_id(0) % steps_per_seq) * tm
    sub8 = lax.broadcasted_iota(jnp.int32, (V7X_SUBLANES, D_RNN), 0)
    slabs = _slabs(D_MODEL)

    seg_lane = lax.broadcasted_iota(jnp.int32, (SUB_ROWS, HEAD_DIM), 1)

    def rope(y, tabs, mul):
        c, s_lo, s_hi = (tabs[:, sl] if mul == 1.0 else tabs[:, sl] * mul for sl in _slabs(3 * HEAD_DIM))
        ts = [y[:, sl] for sl in _slabs(GROUP_WIDTH)]
        packed = ts[0]
        for hd in range(1, HEADS_PER_GROUP):
            packed = jnp.where(seg_lane >= hd * ROPE_DIM, pltpu.roll(ts[hd], hd * ROPE_DIM, axis=1), packed)
        up = pltpu.roll(packed, HEAD_DIM - half, axis=1)
        dn = pltpu.roll(packed, half, axis=1)
        rot = packed * c + up * s_lo + dn * s_hi
        outs = []
        for hd, t in enumerate(ts):
            back = rot if hd == 0 else pltpu.roll(rot, HEAD_DIM - hd * ROPE_DIM, axis=1)
            outs.append(jnp.where(seg_lane < ROPE_DIM, back, t if mul == 1.0 else t * mul))
        return jnp.concatenate(outs, axis=1)

    def chunk(lhs_ref, ci):
        return jnp.dot(lhs_ref[...], w_ref[:, ci * IN_CHUNK:(ci + 1) * IN_CHUNK], preferred_element_type=F32)

    def wide(lhs, first_chunk):
        return jnp.concatenate([chunk(lhs, first_chunk + j) for j in range(D_RNN // IN_CHUNK)], axis=1)

    nl = -lam_ref[...]
    softplus = jnp.maximum(nl, 0.0) + jnp.log1p(jnp.exp(-jnp.abs(nl)))
    log2_a_coef = (-LRU_C * math.log2(math.e)) * softplus

    @pl.when(seq_pos == 0)
    def _():
        xr_halo[...] = jnp.zeros(xr_halo.shape, F32)
        hstate[...] = jnp.zeros(hstate.shape, F32)

    n_sub = tm // SUB_ROWS
    dyn_zero = jnp.minimum(pl.program_id(0), 0)

    def emit(sub, ref, y, d):
        n = SUB_ROWS // d
        for r in range(d):
            ref[r, sub * n:(sub + 1) * n, :] = y[r * n:(r + 1) * n, :].astype(ref.dtype)

    def qkv(sub, g):
        d, lhs_g = DILATIONS[g], lhs_scr.at[sub, 1 + g]
        tabs = _residue_major(rope_scr.at[sub], d, SUB_ROWS)
        emit(sub, (q1_ref, q2_ref, q3_ref)[g], rope(chunk(lhs_g, q_base + g), tabs, scale), d)
        emit(sub, (k1_ref, k2_ref, k3_ref)[g], rope(chunk(lhs_g, q_base + N_GROUPS + g), tabs, 1.0), d)
        emit(sub, (v1_ref, v2_ref, v3_ref)[g], chunk(lhs_g, q_base + 2 * N_GROUPS + g), d)

    for sub in range(n_sub):
        rows = slice(sub * SUB_ROWS, (sub + 1) * SUB_ROWS)
        h_sub, rope_sub = h_scr.at[sub], rope_scr.at[sub]
        h = _rms_norm(x_ref[rows, :], g_ref[...])
        for c, sl in enumerate(slabs):
            h_sub[c] = h[:, sl]
        for c in range(N_CHUNKS):
            for si, sl in enumerate(slabs):
                hl_scr[si, pl.ds(c, CHUNK_LEN, stride=N_CHUNKS), :] = h[c * CHUNK_LEN:(c + 1) * CHUNK_LEN, sl]
        lhs = lhs_scr.at[sub]
        lhs[0] = jnp.concatenate([hl_scr[si] for si in range(len(slabs))], axis=1).astype(BF16)
        lhs[1] = _residue_major(h_sub, DILATIONS[0], SUB_ROWS).astype(BF16)
        h_d4 = _residue_major(h_sub, DILATIONS[1], SUB_ROWS)
        lhs[2] = h_d4.astype(BF16)
        for si, sl in enumerate(slabs):
            h4_scr[si] = h_d4[:, sl]
        lhs[3] = _regroup(h4_scr, DILATIONS[1], DILATIONS[2], SUB_ROWS).astype(BF16)

        ang = freq_ref[...] * pos_ref[:, rows].astype(F32)
        cos_t, sin_t = jnp.cos(ang), jnp.sin(ang)
        zeros_t = jnp.zeros((half, SUB_ROWS), F32)
        per_head = lambda t: jnp.concatenate([t] * HEADS_PER_GROUP, axis=0).T
        rope_sub[0] = per_head(cos_t)
        rope_sub[1] = per_head(jnp.concatenate([-sin_t[:half], zeros_t], axis=0))
        rope_sub[2] = per_head(jnp.concatenate([zeros_t, sin_t[half:]], axis=0))

        xr = wide(lhs.at[0], 0)
        gr = wide(lhs.at[0], D_RNN // IN_CHUNK)
        qkv(sub, 0)

        prev = xr_halo[...]
        wraps = []
        for j in range(CONV_WIDTH - 1):
            grp = slice(j * V7X_SUBLANES, (j + 1) * V7X_SUBLANES)
            cur = xr[SUB_ROWS - CONV_TAIL + j * V7X_SUBLANES:SUB_ROWS - CONV_TAIL + (j + 1) * V7X_SUBLANES, :]
            wraps.append(jnp.where(sub8 == 0, pltpu.roll(prev[grp, :], 1, axis=0), pltpu.roll(cur, 1, axis=0)))
        xr_halo[...] = xr[SUB_ROWS - CONV_TAIL:, :]
        xe = jnp.concatenate(wraps + [xr], axis=0)
        y = cb_ref[...]
        for s in range(CONV_WIDTH):
            lo = CONV_TAIL - s * V7X_SUBLANES
            y = y + xe[lo:lo + SUB_ROWS, :] * cw_ref[CONV_WIDTH - 1 - s:CONV_WIDTH - s, :]
        yb = y.astype(BF16)

        for j in range(2 * D_MODEL // IN_CHUNK):
            cols = slice(j * IN_CHUNK, (j + 1) * IN_CHUNK)
            gates_ref[rows, cols] = chunk(lhs.at[1], gates_base + j).astype(gates_ref.dtype)
        qkv(sub, 1)
        pre = gate_scr.at[dyn_zero]
        for j in range(N_LRU_TILES):
            sl = slice(j * V7X_MXU_DIM, (j + 1) * V7X_MXU_DIM)
            pre[0, :, sl] = jnp.dot(yb[:, sl], wrg_ref[j], preferred_element_type=F32)
            pre[1, :, sl] = jnp.dot(yb[:, sl], wig_ref[j], preferred_element_type=F32)
        for g in range(2, N_GROUPS):
            qkv(sub, g)

        r = _sigmoid(pre[0] + brg_ref[...])
        ig = _sigmoid(pre[1] + big_ref[...])
        a = jnp.exp2(log2_a_coef * r)
        mult = jnp.sqrt(1.0 - a * a)
        if sub == 0:
            row = seq_pos + lax.broadcasted_iota(jnp.int32, (SUB_ROWS, 1), 0)
            mult = jnp.where(row == 0, 1.0, mult)
        u = mult * (ig * y)

        h_loc, p_loc = [], []
        for i in range(CHUNK_LEN):
            grp = slice(i * V7X_SUBLANES, (i + 1) * V7X_SUBLANES)
            h_loc.append(u[grp, :] if i == 0 else a[grp, :] * h_loc[-1] + u[grp, :])
            p_loc.append(a[grp, :] if i == 0 else a[grp, :] * p_loc[-1])
        h_in = jnp.where(sub8 == 0, pltpu.roll(hstate[...], 1, axis=0), 0.0)
        for c in range(1, N_CHUNKS):
            ends = h_loc[-1] + p_loc[-1] * h_in
            h_in = jnp.where(sub8 == c, pltpu.roll(ends, 1, axis=0), h_in)
        hstate[...] = h_loc[-1] + p_loc[-1] * h_in

        for i in range(CHUNK_LEN):
            grp = slice(i * V7X_SUBLANES, (i + 1) * V7X_SUBLANES)
            out = (h_loc[i] + p_loc[i] * h_in) * _gelu_tanh(gr[grp, :])
            for si, sl in enumerate(slabs):
                g_scr[si, grp, :] = out[:, sl]
        for c in range(N_CHUNKS):
            tok = slice(sub * SUB_ROWS + c * CHUNK_LEN, sub * SUB_ROWS + (c + 1) * CHUNK_LEN)
            gated_ref[tok, :] = jnp.concatenate(
                [g_scr[si, pl.ds(c, CHUNK_LEN, stride=N_CHUNKS), :] for si in range(len(slabs))],
                axis=1).astype(gated_ref.dtype)


def _grouped_spec(d, rows, width, steps_per_seq):
    return pl.BlockSpec((None, d, rows // d, width),
                        lambda i: (i // steps_per_seq, 0, i % steps_per_seq, 0))


def _in_proj_lru(x2, pos2, g, freq, w_in, conv_w, conv_b, wrg, brg, wig, big, lam, B, S, tm):
    T = B * S
    nt = S // tm
    n_sub = tm // SUB_ROWS
    row = lambda w: pl.BlockSpec((tm, w), lambda i: (i, 0))
    qkv_specs = [_grouped_spec(d, tm, GROUP_WIDTH, nt) for d in DILATIONS] * 3
    qkv_shapes = [jax.ShapeDtypeStruct((B, d, S // d, GROUP_WIDTH), BF16) for d in DILATIONS] * 3
    flat = lambda w: jax.ShapeDtypeStruct((T, w), BF16)
    vec = _const_spec((1, D_RNN))
    gate_w = _const_spec((N_LRU_TILES, V7X_MXU_DIM, V7X_MXU_DIM))
    return pl.pallas_call(
        functools.partial(_in_proj_lru_kernel, tm=tm, steps_per_seq=nt),
        grid=(T // tm,),
        in_specs=[row(D_MODEL), pl.BlockSpec((None, 1, tm), lambda i: (i, 0, 0)),
                  _const_spec((1, D_MODEL)), _const_spec((ROPE_DIM, 1)),
                  _const_spec((D_MODEL, IN_WIDTH)),
                  _const_spec((CONV_WIDTH, D_RNN)), vec, gate_w, vec, gate_w, vec, vec],
        out_specs=[row(D_RNN)] + qkv_specs + [row(2 * D_MODEL)],
        out_shape=[flat(D_RNN)] + qkv_shapes + [flat(2 * D_MODEL)],
        scratch_shapes=[pltpu.VMEM((n_sub, D_MODEL // V7X_LANES, SUB_ROWS, V7X_LANES), F32),
                        pltpu.VMEM((n_sub, 3, SUB_ROWS, V7X_LANES), F32),
                        pltpu.VMEM((D_MODEL // V7X_LANES, SUB_ROWS, V7X_LANES), F32),
                        pltpu.VMEM((D_MODEL // V7X_LANES, SUB_ROWS, V7X_LANES), F32),
                        pltpu.VMEM((n_sub, 1 + N_GROUPS, SUB_ROWS, D_MODEL), BF16),
                        pltpu.VMEM((1, 2, SUB_ROWS, D_RNN), F32),
                        pltpu.VMEM((CONV_TAIL, D_RNN), F32),
                        pltpu.VMEM((V7X_SUBLANES, D_RNN), F32),
                        pltpu.VMEM((D_RNN // V7X_LANES, SUB_ROWS, V7X_LANES), F32)],
        compiler_params=pltpu.CompilerParams(
            dimension_semantics=("arbitrary",), vmem_limit_bytes=VMEM_LIMIT_BYTES),
        name="in_proj_lru",
    )(x2, pos2, g, freq, w_in, conv_w, conv_b, wrg, brg, wig, big, lam)


def _attn_kernel(q_ref, k_ref, v_ref, o_ref, lse_ref, *, n_res, n_blk):
    qi = lax.broadcasted_iota(jnp.int32, (Q_BLOCK, 2 * Q_BLOCK), 0)
    kj = lax.broadcasted_iota(jnp.int32, (Q_BLOCK, 2 * Q_BLOCK), 1)
    band2 = (kj >= qi) & (kj <= qi + Q_BLOCK)
    band1 = (lax.broadcasted_iota(jnp.int32, (Q_BLOCK, Q_BLOCK), 1)
             <= lax.broadcasted_iota(jnp.int32, (Q_BLOCK, Q_BLOCK), 0))
    lane = lax.broadcasted_iota(jnp.int32, (Q_BLOCK, HEAD_DIM), 1)
    seg = HEAD_DIM // HEADS_PER_GROUP

    def rows_of(n):
        q_rows = slice(n * Q_BLOCK, (n + 1) * Q_BLOCK)
        kv_rows = slice(max(n - 1, 0) * Q_BLOCK, (n + 1) * Q_BLOCK)
        return q_rows, kv_rows

    def scores(r, n):
        q_rows, kv_rows = rows_of(n)
        band = band1 if n == 0 else band2
        out = []
        for cols in _slabs(GROUP_WIDTH):
            s = lax.dot_general(q_ref[r, q_rows, cols], k_ref[r, kv_rows, cols],
                                (((1,), (1,)), ((), ())), preferred_element_type=F32)
            out.append(jnp.where(band, s, NEG))
        return out

    def finish(r, n, s_list):
        q_rows, kv_rows = rows_of(n)
        ps, dens = [], []
        lse_tile = jnp.zeros((Q_BLOCK, HEAD_DIM), F32)
        for hd, s in enumerate(s_list):
            m = jnp.max(s, axis=-1, keepdims=True)
            p = jnp.exp(s - m)
            den = jnp.sum(p, axis=-1, keepdims=True)
            ps.append(p.astype(BF16))
            dens.append(den)
            lse_tile = jnp.where(lane // seg == hd, m + jnp.log(den), lse_tile)
        lse_ref[r, q_rows, :] = lse_tile
        for hd, cols in enumerate(_slabs(GROUP_WIDTH)):
            o = jnp.dot(ps[hd], v_ref[r, kv_rows, cols], preferred_element_type=F32) / dens[hd]
            o_ref[r, q_rows, cols] = o.astype(o_ref.dtype)

    blocks = [(r, n) for r in range(n_res) for n in range(n_blk)]
    pending = None
    for blk in blocks:
        s_list = scores(*blk)
        if pending is not None:
            finish(*pending)
        pending = (*blk, s_list)
    finish(*pending)


def _attention_group(q, k, v):
    B, d, L, _ = q.shape
    qkv_spec = pl.BlockSpec((None, d, L, GROUP_WIDTH), lambda b: (b, 0, 0, 0))
    lse_spec = pl.BlockSpec((None, d, L, HEAD_DIM), lambda b: (b, 0, 0, 0))
    return pl.pallas_call(
        functools.partial(_attn_kernel, n_res=d, n_blk=L // Q_BLOCK),
        grid=(B,),
        in_specs=[qkv_spec, qkv_spec, qkv_spec],
        out_specs=[qkv_spec, lse_spec],
        out_shape=[jax.ShapeDtypeStruct((B, d, L, GROUP_WIDTH), BF16),
                   jax.ShapeDtypeStruct((B, d, L, HEAD_DIM), F32)],
        compiler_params=pltpu.CompilerParams(
            dimension_semantics=("arbitrary",), vmem_limit_bytes=VMEM_LIMIT_BYTES),
        name=f"attention_d{d}",
    )(q, k, v)


FF_CHUNK = 2 * V7X_MXU_DIM


def _merge_ffn_kernel(x_ref, gated_ref, o1_ref, o2_ref, o3_ref, l1_ref, l2_ref, l3_ref, gates_ref,
                      wl_ref, wa_ref, wo_ref, gpost_ref, gpre_ref, wg_ref, wu_ref, wd_ref, gffn_ref,
                      out_ref, act_ref, o_scr, l_scr, x1_scr, *, tm):
    seg = HEAD_DIM // HEADS_PER_GROUP

    def mix_phase(sub):
        rows = slice(sub * SUB_ROWS, (sub + 1) * SUB_ROWS)
        y_lru = jnp.dot(gated_ref[rows, :], wl_ref[...], preferred_element_type=F32)

        for g, (d, o_ref, l_ref) in enumerate(zip(DILATIONS, (o1_ref, o2_ref, o3_ref), (l1_ref, l2_ref, l3_ref))):
            n = SUB_ROWS // d
            src = slice(sub * n, (sub + 1) * n)
            for r in range(d):
                dst = pl.ds(r, n, stride=d) if d > 1 else pl.ds(0, n)
                l_scr[sub, g, dst, :] = l_ref[r, src, :]
                for c, sl in enumerate(_slabs(GROUP_WIDTH)):
                    o_scr[sub, g, c, dst, :] = o_ref[r, src, sl].astype(F32)

        ls = [l_scr[sub, g] for g in range(N_GROUPS)]
        mx = jnp.maximum(jnp.maximum(ls[0], ls[1]), ls[2])
        es = [jnp.exp(l - mx) for l in ls]
        inv = 1.0 / (es[0] + es[1] + es[2])
        heads = []
        for hd in range(HEADS_PER_GROUP):
            acc = None
            for g in range(N_GROUPS):
                w = (es[g] * inv)[:, hd * seg:hd * seg + 1]
                term = w * o_scr[sub, g, hd]
                acc = term if acc is None else acc + term
            heads.append(acc)
        o = jnp.concatenate(heads, axis=1).astype(BF16)
        y_attn = jnp.dot(o, wa_ref[...], preferred_element_type=F32)

        g_lru = _sigmoid(gates_ref[rows, :D_MODEL].astype(F32))
        g_attn = _sigmoid(gates_ref[rows, D_MODEL:].astype(F32))
        merged = (g_lru * y_lru + g_attn * y_attn).astype(BF16)
        mix = jnp.dot(merged, wo_ref[...], preferred_element_type=F32)
        x1 = x_ref[rows, :] + _rms_norm(mix, gpost_ref[...])
        x1_scr[sub] = x1
        return _rms_norm(x1, gpre_ref[...]).astype(BF16)

    def ffn_phase(sub, h):
        rows = slice(sub * SUB_ROWS, (sub + 1) * SUB_ROWS)
        for c in range(0, D_FF, FF_CHUNK):
            w = min(FF_CHUNK, D_FF - c)
            gate = jnp.dot(h, wg_ref[:, c:c + w], preferred_element_type=F32)
            up = jnp.dot(h, wu_ref[:, c:c + w], preferred_element_type=F32)
            act_ref[sub, :, c:c + w] = (gate * _sigmoid(gate) * up).astype(BF16)
        f = jnp.dot(act_ref[sub], wd_ref[...], preferred_element_type=F32)
        out_ref[rows, :] = x1_scr[sub] + _rms_norm(f, gffn_ref[...])

    n_sub = tm // SUB_ROWS
    hs = [mix_phase(sub) for sub in range(n_sub)]
    for sub in range(n_sub):
        ffn_phase(sub, hs[sub])


def _merge_ffn(x2, gated, os, ls, gates, wl, wa, wo, gpost, gpre, wg, wu, wd, gffn, S, tm):
    T = x2.shape[0]
    nt = S // tm
    row = lambda w: pl.BlockSpec((tm, w), lambda i: (i, 0))
    vec = _const_spec((1, D_MODEL))
    return pl.pallas_call(
        functools.partial(_merge_ffn_kernel, tm=tm),
        grid=(T // tm,),
        in_specs=[row(D_MODEL), row(D_RNN)]
                 + [_grouped_spec(d, tm, GROUP_WIDTH, nt) for d in DILATIONS]
                 + [_grouped_spec(d, tm, HEAD_DIM, nt) for d in DILATIONS]
                 + [row(2 * D_MODEL),
                    _const_spec((D_RNN, D_MODEL)), _const_spec((GROUP_WIDTH, D_MODEL)),
                    _const_spec((D_MODEL, D_MODEL)), vec, vec,
                    _const_spec((D_MODEL, D_FF)), _const_spec((D_MODEL, D_FF)),
                    _const_spec((D_FF, D_MODEL)), vec],
        out_specs=row(D_MODEL),
        out_shape=jax.ShapeDtypeStruct((T, D_MODEL), F32),
        scratch_shapes=[pltpu.VMEM((tm // SUB_ROWS, SUB_ROWS, D_FF), BF16),
                        pltpu.VMEM((tm // SUB_ROWS, N_GROUPS, HEADS_PER_GROUP, SUB_ROWS, HEAD_DIM), F32),
                        pltpu.VMEM((tm // SUB_ROWS, N_GROUPS, SUB_ROWS, HEAD_DIM), F32),
                        pltpu.VMEM((tm // SUB_ROWS, SUB_ROWS, D_MODEL), F32)],
        compiler_params=pltpu.CompilerParams(
            dimension_semantics=("arbitrary",), vmem_limit_bytes=VMEM_LIMIT_BYTES),
        name="merge_ffn",
    )(x2, gated, *os, *ls, gates, wl, wa, wo, gpost, gpre, wg, wu, wd, gffn)


def _pack_lru_gate(w):
    w4 = w.reshape(N_LRU_TILES, LRU_PACK, LRU_BLOCK, LRU_BLOCK)
    eye = jnp.eye(LRU_PACK, dtype=w.dtype)
    packed = jnp.einsum('jacd,ab->jacbd', w4, eye)
    return packed.reshape(N_LRU_TILES, V7X_MXU_DIM, V7X_MXU_DIM).astype(BF16)


def kernel(x, positions, pre_mix_norm, w_in, conv_w, conv_b, w_rg, b_rg, w_ig, b_ig, lru_lambda,
           w_lru_proj, w_attn_proj, w_out, post_mix_norm, pre_ffn_norm, w_ffn_gate, w_ffn_up,
           w_ffn_down, post_ffn_norm):
    B, S, D = x.shape
    assert D == D_MODEL and S % TM == 0 and TM % SUB_ROWS == 0
    assert SUB_ROWS % (DILATIONS[-1] * 2 * V7X_SUBLANES) == 0
    assert pre_mix_norm.shape[0] == 1, "single-layer block"
    T = B * S

    inv_freq = ROPE_THETA ** (-jnp.arange(0, ROPE_DIM, 2, dtype=F32) / ROPE_DIM)
    freq = jnp.concatenate([inv_freq, inv_freq])[:, None]

    x2 = x.reshape(T, D)
    pos2 = positions.reshape(T // TM, 1, TM)
    row = lambda p: p[0][None, :]

    gated, q1, q2, q3, k1, k2, k3, v1, v2, v3, gates = _in_proj_lru(
        x2, pos2, row(pre_mix_norm), freq, w_in[0].astype(BF16), conv_w[0], row(conv_b),
        _pack_lru_gate(w_rg[0]), row(b_rg), _pack_lru_gate(w_ig[0]), row(b_ig), row(lru_lambda),
        B, S, TM)

    os, ls = [], []
    for q, k, v in ((q1, k1, v1), (q2, k2, v2), (q3, k3, v3)):
        o, l = _attention_group(q, k, v)
        os.append(o)
        ls.append(l)

    out = _merge_ffn(x2, gated, os, ls, gates,
                     w_lru_proj[0].astype(BF16), w_attn_proj[0].astype(BF16), w_out[0].astype(BF16),
                     row(post_mix_norm), row(pre_ffn_norm), w_ffn_gate[0].astype(BF16),
                     w_ffn_up[0].astype(BF16), w_ffn_down[0].astype(BF16), row(post_ffn_norm), S, TM)
    return out.reshape(B, S, D)
```

```python
import functools
import math

import jax
import jax.numpy as jnp
from jax import lax
from jax.experimental import pallas as pl
from jax.experimental.pallas import tpu as pltpu

D_MODEL = 1024
D_RNN = 1024
N_LRU_BLOCKS = 16
LRU_BLOCK = D_RNN // N_LRU_BLOCKS
CONV_WIDTH = 4
LRU_C = 8.0
ATTN_GROUPS = ((128, 1), (512, 4), (2048, 16))
DILATIONS = tuple(d for _, d in ATTN_GROUPS)
N_GROUPS = len(ATTN_GROUPS)
HEADS_PER_GROUP = 4
HEAD_DIM = 128
GROUP_WIDTH = HEADS_PER_GROUP * HEAD_DIM
ATTN_WIDTH = N_GROUPS * GROUP_WIDTH
ROPE_DIM = HEAD_DIM // 4
ROPE_THETA = 500000.0
Q_BLOCK = 128
D_FF = ((8 * D_MODEL // 3 + 255) // 256) * 256
IN_WIDTH = 2 * D_RNN + 3 * ATTN_WIDTH + 2 * D_MODEL
EPS = 1e-6
NEG = -1e30

V7X_LANES = 128
V7X_SUBLANES = 8
V7X_MXU_DIM = 256
V7X_VMEM_BYTES = 64 * 1024 * 1024
VMEM_LIMIT_BYTES = V7X_VMEM_BYTES - 8 * 1024 * 1024

BF16 = jnp.bfloat16
F32 = jnp.float32

LRU_PACK = V7X_MXU_DIM // LRU_BLOCK
N_LRU_TILES = N_LRU_BLOCKS // LRU_PACK

TM = 512
SUB_ROWS = 256
N_CHUNKS = V7X_SUBLANES
CHUNK_LEN = SUB_ROWS // N_CHUNKS
CONV_TAIL = (CONV_WIDTH - 1) * V7X_SUBLANES

assert all(w // d == Q_BLOCK for w, d in ATTN_GROUPS), "band logic assumes window == dilation * Q_BLOCK"
assert CONV_WIDTH - 1 <= CHUNK_LEN


def _rms_norm(x, g):
    return x * lax.rsqrt(jnp.mean(x * x, axis=-1, keepdims=True) + EPS) * g


def _sigmoid(x):
    return 0.5 * jnp.tanh(0.5 * x) + 0.5


def _gelu_tanh(x):
    c = math.sqrt(2.0 / math.pi)
    return x * (0.5 * jnp.tanh(x * (c + (c * 0.044715) * (x * x))) + 0.5)


def _const_spec(shape):
    nd = len(shape)
    return pl.BlockSpec(shape, lambda *_: (0,) * nd, pipeline_mode=pl.Buffered(1))


def _slabs(width):
    return [slice(c * V7X_LANES, (c + 1) * V7X_LANES) for c in range(width // V7X_LANES)]


def _residue_major(slab_ref, d, n_rows):
    if d == 1:
        return jnp.concatenate([slab_ref[c] for c in range(slab_ref.shape[0])], axis=1)
    n = n_rows // d
    return jnp.concatenate(
        [jnp.concatenate([slab_ref[c, pl.ds(r, n, stride=d), :] for c in range(slab_ref.shape[0])], axis=1)
         for r in range(d)], axis=0)


def _regroup(slab_ref, d_in, d_out, n_rows):
    assert d_out == d_in * d_in
    n_in, n_out = n_rows // d_in, n_rows // d_out
    pieces = []
    for r in range(d_out):
        start = (r % d_in) * n_in + r // d_in
        pieces.append(jnp.concatenate(
            [slab_ref[c, pl.ds(start, n_out, stride=d_in), :] for c in range(slab_ref.shape[0])], axis=1))
    return jnp.concatenate(pieces, axis=0)


IN_CHUNK = GROUP_WIDTH


def _in_proj_lru_kernel(x_ref, pos_ref, g_ref, freq_ref, w_ref,
                        cw_ref, cb_ref, wrg_ref, brg_ref, wig_ref, big_ref, lam_ref,
                        gated_ref, q1_ref, q2_ref, q3_ref, k1_ref, k2_ref, k3_ref,
                        v1_ref, v2_ref, v3_ref, gates_ref,
                        h_scr, rope_scr, hl_scr, h4_scr, lhs_scr, gate_scr, xr_halo, hstate, g_scr, *, tm, steps_per_seq):
    half = ROPE_DIM // 2
    scale = HEAD_DIM ** -0.5
    q_base = 2 * D_RNN // IN_CHUNK
    gates_base = q_base + 3 * N_GROUPS
    seq_pos = (pl.program_id(0) % steps_per_seq) * tm
    sub8 = lax.broadcasted_iota(jnp.int32, (V7X_SUBLANES, D_RNN), 0)
    slabs = _slabs(D_MODEL)

    seg_lane = lax.broadcasted_iota(jnp.int32, (SUB_ROWS, HEAD_DIM), 1)

    def rope(y, tabs, mul):
        c, s_lo, s_hi = (tabs[:, sl] if mul == 1.0 else tabs[:, sl] * mul for sl in _slabs(3 * HEAD_DIM))
        ts = [y[:, sl] for sl in _slabs(GROUP_WIDTH)]
        packed = ts[0]
        for hd in range(1, HEADS_PER_GROUP):
            packed = jnp.where(seg_lane >= hd * ROPE_DIM, pltpu.roll(ts[hd], hd * ROPE_DIM, axis=1), packed)
        up = pltpu.roll(packed, HEAD_DIM - half, axis=1)
        dn = pltpu.roll(packed, half, axis=1)
        rot = packed * c + up * s_lo + dn * s_hi
        outs = []
        for hd, t in enumerate(ts):
            back = rot if hd == 0 else pltpu.roll(rot, HEAD_DIM - hd * ROPE_DIM, axis=1)
            outs.append(jnp.where(seg_lane < ROPE_DIM, back, t if mul == 1.0 else t * mul))
        return jnp.concatenate(outs, axis=1)

    def chunk(lhs_ref, ci):
        return jnp.dot(lhs_ref[...], w_ref[:, ci * IN_CHUNK:(ci + 1) * IN_CHUNK], preferred_element_type=F32)

    def wide(lhs, first_chunk):
        return jnp.concatenate([chunk(lhs, first_chunk + j) for j in range(D_RNN // IN_CHUNK)], axis=1)

    nl = -lam_ref[...]
    softplus = jnp.maximum(nl, 0.0) + jnp.log1p(jnp.exp(-jnp.abs(nl)))
    log2_a_coef = (-LRU_C * math.log2(math.e)) * softplus

    @pl.when(seq_pos == 0)
    def _():
        xr_halo[...] = jnp.zeros(xr_halo.shape, F32)
        hstate[...] = jnp.zeros(hstate.shape, F32)

    n_sub = tm // SUB_ROWS
    dyn_zero = jnp.minimum(pl.program_id(0), 0)

    def emit(sub, ref, y, d):
        n = SUB_ROWS // d
        for r in range(d):
            ref[r, sub * n:(sub + 1) * n, :] = y[r * n:(r + 1) * n, :].astype(ref.dtype)

    def qkv(sub, g):
        d, lhs_g = DILATIONS[g], lhs_scr.at[sub, 1 + g]
        tabs = _residue_major(rope_scr.at[sub], d, SUB_ROWS)
        emit(sub, (q1_ref, q2_ref, q3_ref)[g], rope(chunk(lhs_g, q_base + g), tabs, scale), d)
        emit(sub, (k1_ref, k2_ref, k3_ref)[g], rope(chunk(lhs_g, q_base + N_GROUPS + g), tabs, 1.0), d)
        emit(sub, (v1_ref, v2_ref, v3_ref)[g], chunk(lhs_g, q_base + 2 * N_GROUPS + g), d)

    for sub in range(n_sub):
        rows = slice(sub * SUB_ROWS, (sub + 1) * SUB_ROWS)
        h_sub, rope_sub = h_scr.at[sub], rope_scr.at[sub]
        h = _rms_norm(x_ref[rows, :], g_ref[...])
        for c, sl in enumerate(slabs):
            h_sub[c] = h[:, sl]
        for c in range(N_CHUNKS):
            for si, sl in enumerate(slabs):
                hl_scr[si, pl.ds(c, CHUNK_LEN, stride=N_CHUNKS), :] = h[c * CHUNK_LEN:(c + 1) * CHUNK_LEN, sl]
        lhs = lhs_scr.at[sub]
        lhs[0] = jnp.concatenate([hl_scr[si] for si in range(len(slabs))], axis=1).astype(BF16)
        lhs[1] = _residue_major(h_sub, DILATIONS[0], SUB_ROWS).astype(BF16)
        h_d4 = _residue_major(h_sub, DILATIONS[1], SUB_ROWS)
        lhs[2] = h_d4.astype(BF16)
        for si, sl in enumerate(slabs):
            h4_scr[si] = h_d4[:, sl]
        lhs[3] = _regroup(h4_scr, DILATIONS[1], DILATIONS[2], SUB_ROWS).astype(BF16)

        ang = freq_ref[...] * pos_ref[:, rows].astype(F32)
        cos_t, sin_t = jnp.cos(ang), jnp.sin(ang)
        zeros_t = jnp.zeros((half, SUB_ROWS), F32)
        per_head = lambda t: jnp.concatenate([t] * HEADS_PER_GROUP, axis=0).T
        rope_sub[0] = per_head(jnp.concatenate([cos_t, cos_t], axis=0))
        rope_sub[1] = per_head(jnp.concatenate([-sin_t, zeros_t], axis=0))
        rope_sub[2] = per_head(jnp.concatenate([zeros_t, sin_t], axis=0))

        xr = wide(lhs.at[0], 0)
        gr = wide(lhs.at[0], D_RNN // IN_CHUNK)
        qkv(sub, 0)

        prev = xr_halo[...]
        wraps = []
        for j in range(CONV_WIDTH - 1):
            grp = slice(j * V7X_SUBLANES, (j + 1) * V7X_SUBLANES)
            cur = xr[SUB_ROWS - CONV_TAIL + j * V7X_SUBLANES:SUB_ROWS - CONV_TAIL + (j + 1) * V7X_SUBLANES, :]
            wraps.append(jnp.where(sub8 == 0, pltpu.roll(prev[grp, :], 1, axis=0), pltpu.roll(cur, 1, axis=0)))
        xr_halo[...] = xr[SUB_ROWS - CONV_TAIL:, :]
        xe = jnp.concatenate(wraps + [xr], axis=0)
        y = cb_ref[...]
        for s in range(CONV_WIDTH):
            lo = CONV_TAIL - s * V7X_SUBLANES
            y = y + xe[lo:lo + SUB_ROWS, :] * cw_ref[CONV_WIDTH - 1 - s:CONV_WIDTH - s, :]
        yb = y.astype(BF16)

        for j in range(2 * D_MODEL // IN_CHUNK):
            cols = slice(j * IN_CHUNK, (j + 1) * IN_CHUNK)
            gates_ref[rows, cols] = chunk(lhs.at[1], gates_base + j).astype(gates_ref.dtype)
        qkv(sub, 1)
        pre = gate_scr.at[dyn_zero]
        for j in range(N_LRU_TILES):
            sl = slice(j * V7X_MXU_DIM, (j + 1) * V7X_MXU_DIM)
            pre[0, :, sl] = jnp.dot(yb[:, sl], wrg_ref[j], preferred_element_type=F32)
            pre[1, :, sl] = jnp.dot(yb[:, sl], wig_ref[j], preferred_element_type=F32)
        for g in range(2, N_GROUPS):
            qkv(sub, g)

        r = _sigmoid(pre[0] + brg_ref[...])
        ig = _sigmoid(pre[1] + big_ref[...])
        a = jnp.exp2(log2_a_coef * r)
        gap = 1.0 - a * a
        mult = jnp.where(gap > 0.0, gap * lax.rsqrt(gap), 0.0)
        if sub == 0:
            row = seq_pos + lax.broadcasted_iota(jnp.int32, (SUB_ROWS, 1), 0)
            mult = jnp.where(row == 0, 1.0, mult)
        u = mult * (ig * y)

        h_loc, p_loc = [], []
        for i in range(CHUNK_LEN):
            grp = slice(i * V7X_SUBLANES, (i + 1) * V7X_SUBLANES)
            h_loc.append(u[grp, :] if i == 0 else a[grp, :] * h_loc[-1] + u[grp, :])
            p_loc.append(a[grp, :] if i == 0 else a[grp, :] * p_loc[-1])
        h_in = jnp.where(sub8 == 0, pltpu.roll(hstate[...], 1, axis=0), 0.0)
        for c in range(1, N_CHUNKS):
            ends = h_loc[-1] + p_loc[-1] * h_in
            h_in = jnp.where(sub8 == c, pltpu.roll(ends, 1, axis=0), h_in)
        hstate[...] = h_loc[-1] + p_loc[-1] * h_in

        for i in range(CHUNK_LEN):
            grp = slice(i * V7X_SUBLANES, (i + 1) * V7X_SUBLANES)
            out = (h_loc[i] + p_loc[i] * h_in) * _gelu_tanh(gr[grp, :])
            for si, sl in enumerate(slabs):
                g_scr[si, grp, :] = out[:, sl]
        for c in range(N_CHUNKS):
            tok = slice(sub * SUB_ROWS + c * CHUNK_LEN, sub * SUB_ROWS + (c + 1) * CHUNK_LEN)
            gated_ref[tok, :] = jnp.concatenate(
                [g_scr[si, pl.ds(c, CHUNK_LEN, stride=N_CHUNKS), :] for si in range(len(slabs))],
                axis=1).astype(gated_ref.dtype)


def _grouped_spec(d, rows, width, steps_per_seq):
    return pl.BlockSpec((None, d, rows // d, width),
                        lambda i: (i // steps_per_seq, 0, i % steps_per_seq, 0))


def _in_proj_lru(x2, pos2, g, freq, w_in, conv_w, conv_b, wrg, brg, wig, big, lam, B, S, tm):
    T = B * S
    nt = S // tm
    n_sub = tm // SUB_ROWS
    row = lambda w: pl.BlockSpec((tm, w), lambda i: (i, 0))
    qkv_specs = [_grouped_spec(d, tm, GROUP_WIDTH, nt) for d in DILATIONS] * 3
    qkv_shapes = [jax.ShapeDtypeStruct((B, d, S // d, GROUP_WIDTH), BF16) for d in DILATIONS] * 3
    flat = lambda w: jax.ShapeDtypeStruct((T, w), BF16)
    vec = _const_spec((1, D_RNN))
    gate_w = _const_spec((N_LRU_TILES, V7X_MXU_DIM, V7X_MXU_DIM))
    return pl.pallas_call(
        functools.partial(_in_proj_lru_kernel, tm=tm, steps_per_seq=nt),
        grid=(T // tm,),
        in_specs=[row(D_MODEL), pl.BlockSpec((None, 1, tm), lambda i: (i, 0, 0)),
                  _const_spec((1, D_MODEL)), _const_spec((ROPE_DIM // 2, 1)),
                  _const_spec((D_MODEL, IN_WIDTH)),
                  _const_spec((CONV_WIDTH, D_RNN)), vec, gate_w, vec, gate_w, vec, vec],
        out_specs=[row(D_RNN)] + qkv_specs + [row(2 * D_MODEL)],
        out_shape=[flat(D_RNN)] + qkv_shapes + [flat(2 * D_MODEL)],
        scratch_shapes=[pltpu.VMEM((n_sub, D_MODEL // V7X_LANES, SUB_ROWS, V7X_LANES), F32),
                        pltpu.VMEM((n_sub, 3, SUB_ROWS, V7X_LANES), F32),
                        pltpu.VMEM((D_MODEL // V7X_LANES, SUB_ROWS, V7X_LANES), F32),
                        pltpu.VMEM((D_MODEL // V7X_LANES, SUB_ROWS, V7X_LANES), F32),
                        pltpu.VMEM((n_sub, 1 + N_GROUPS, SUB_ROWS, D_MODEL), BF16),
                        pltpu.VMEM((1, 2, SUB_ROWS, D_RNN), F32),
                        pltpu.VMEM((CONV_TAIL, D_RNN), F32),
                        pltpu.VMEM((V7X_SUBLANES, D_RNN), F32),
                        pltpu.VMEM((D_RNN // V7X_LANES, SUB_ROWS, V7X_LANES), F32)],
        compiler_params=pltpu.CompilerParams(
            dimension_semantics=("arbitrary",), vmem_limit_bytes=VMEM_LIMIT_BYTES),
        name="in_proj_lru",
    )(x2, pos2, g, freq, w_in, conv_w, conv_b, wrg, brg, wig, big, lam)


def _attn_kernel(q_ref, k_ref, v_ref, o_ref, lse_ref, *, n_res, n_blk):
    qi = lax.broadcasted_iota(jnp.int32, (Q_BLOCK, 2 * Q_BLOCK), 0)
    kj = lax.broadcasted_iota(jnp.int32, (Q_BLOCK, 2 * Q_BLOCK), 1)
    band2 = (kj >= qi) & (kj <= qi + Q_BLOCK)
    band1 = (lax.broadcasted_iota(jnp.int32, (Q_BLOCK, Q_BLOCK), 1)
             <= lax.broadcasted_iota(jnp.int32, (Q_BLOCK, Q_BLOCK), 0))
    lane = lax.broadcasted_iota(jnp.int32, (Q_BLOCK, HEAD_DIM), 1)
    seg = HEAD_DIM // HEADS_PER_GROUP

    def rows_of(n):
        q_rows = slice(n * Q_BLOCK, (n + 1) * Q_BLOCK)
        kv_rows = slice(max(n - 1, 0) * Q_BLOCK, (n + 1) * Q_BLOCK)
        return q_rows, kv_rows

    def scores(r, n):
        q_rows, kv_rows = rows_of(n)
        band = band1 if n == 0 else band2
        out = []
        for cols in _slabs(GROUP_WIDTH):
            s = lax.dot_general(q_ref[r, q_rows, cols], k_ref[r, kv_rows, cols],
                                (((1,), (1,)), ((), ())), preferred_element_type=F32)
            out.append(jnp.where(band, s, NEG))
        return out

    def finish(r, n, s_list):
        q_rows, kv_rows = rows_of(n)
        ps, dens = [], []
        lse_tile = jnp.zeros((Q_BLOCK, HEAD_DIM), F32)
        for hd, s in enumerate(s_list):
            m = jnp.max(s, axis=-1, keepdims=True)
            p = jnp.exp(s - m)
            den = jnp.sum(p, axis=-1, keepdims=True)
            ps.append(p.astype(BF16))
            dens.append(den)
            lse_tile = jnp.where(lane // seg == hd, m + jnp.log(den), lse_tile)
        lse_ref[r, q_rows, :] = lse_tile
        for hd, cols in enumerate(_slabs(GROUP_WIDTH)):
            o = jnp.dot(ps[hd], v_ref[r, kv_rows, cols], preferred_element_type=F32) / dens[hd]
            o_ref[r, q_rows, cols] = o.astype(o_ref.dtype)

    blocks = [(r, n) for r in range(n_res) for n in range(n_blk)]
    pending = None
    for blk in blocks:
        s_list = scores(*blk)
        if pending is not None:
            finish(*pending)
        pending = (*blk, s_list)
    finish(*pending)


def _attention_group(q, k, v):
    B, d, L, _ = q.shape
    qkv_spec = pl.BlockSpec((None, d, L, GROUP_WIDTH), lambda b: (b, 0, 0, 0))
    lse_spec = pl.BlockSpec((None, d, L, HEAD_DIM), lambda b: (b, 0, 0, 0))
    return pl.pallas_call(
        functools.partial(_attn_kernel, n_res=d, n_blk=L // Q_BLOCK),
        grid=(B,),
        in_specs=[qkv_spec, qkv_spec, qkv_spec],
        out_specs=[qkv_spec, lse_spec],
        out_shape=[jax.ShapeDtypeStruct((B, d, L, GROUP_WIDTH), BF16),
                   jax.ShapeDtypeStruct((B, d, L, HEAD_DIM), F32)],
        compiler_params=pltpu.CompilerParams(
            dimension_semantics=("arbitrary",), vmem_limit_bytes=VMEM_LIMIT_BYTES),
        name=f"attention_d{d}",
    )(q, k, v)


FF_CHUNK = 2 * V7X_MXU_DIM


def _merge_ffn_kernel(x_ref, gated_ref, o1_ref, o2_ref, o3_ref, l1_ref, l2_ref, l3_ref, gates_ref,
                      wl_ref, wa_ref, wo_ref, gpost_ref, gpre_ref, wg_ref, wu_ref, wd_ref, gffn_ref,
                      out_ref, act_ref, o_scr, l_scr, x1_scr, *, tm):
    seg = HEAD_DIM // HEADS_PER_GROUP

    def mix_phase(sub):
        rows = slice(sub * SUB_ROWS, (sub + 1) * SUB_ROWS)
        y_lru = jnp.dot(gated_ref[rows, :], wl_ref[...], preferred_element_type=F32)

        for g, (d, o_ref, l_ref) in enumerate(zip(DILATIONS, (o1_ref, o2_ref, o3_ref), (l1_ref, l2_ref, l3_ref))):
            n = SUB_ROWS // d
            src = slice(sub * n, (sub + 1) * n)
            for r in range(d):
                dst = pl.ds(r, n, stride=d) if d > 1 else pl.ds(0, n)
                l_scr[sub, g, dst, :] = l_ref[r, src, :]
                for c, sl in enumerate(_slabs(GROUP_WIDTH)):
                    o_scr[sub, g, c, dst, :] = o_ref[r, src, sl].astype(F32)

        ls = [l_scr[sub, g] for g in range(N_GROUPS)]
        mx = jnp.maximum(jnp.maximum(ls[0], ls[1]), ls[2])
        es = [jnp.exp(l - mx) for l in ls]
        inv = 1.0 / (es[0] + es[1] + es[2])
        heads = []
        for hd in range(HEADS_PER_GROUP):
            acc = None
            for g in range(N_GROUPS):
                w = (es[g] * inv)[:, hd * seg:hd * seg + 1]
                term = w * o_scr[sub, g, hd]
                acc = term if acc is None else acc + term
            heads.append(acc)
        o = jnp.concatenate(heads, axis=1).astype(BF16)
        y_attn = jnp.dot(o, wa_ref[...], preferred_element_type=F32)

        g_lru = _sigmoid(gates_ref[rows, :D_MODEL].astype(F32))
        g_attn = _sigmoid(gates_ref[rows, D_MODEL:].astype(F32))
        merged = (g_lru * y_lru + g_attn * y_attn).astype(BF16)
        mix = jnp.dot(merged, wo_ref[...], preferred_element_type=F32)
        x1 = x_ref[rows, :] + _rms_norm(mix, gpost_ref[...])
        x1_scr[sub] = x1
        return _rms_norm(x1, gpre_ref[...]).astype(BF16)

    def ffn_phase(sub, h):
        rows = slice(sub * SUB_ROWS, (sub + 1) * SUB_ROWS)
        for c in range(0, D_FF, FF_CHUNK):
            w = min(FF_CHUNK, D_FF - c)
            gate = jnp.dot(h, wg_ref[:, c:c + w], preferred_element_type=F32)
            up = jnp.dot(h, wu_ref[:, c:c + w], preferred_element_type=F32)
            act_ref[sub, :, c:c + w] = (gate * _sigmoid(gate) * up).astype(BF16)
        f = jnp.dot(act_ref[sub], wd_ref[...], preferred_element_type=F32)
        out_ref[rows, :] = x1_scr[sub] + _rms_norm(f, gffn_ref[...])

    n_sub = tm // SUB_ROWS
    hs = [mix_phase(sub) for sub in range(n_sub)]
    for sub in range(n_sub):
        ffn_phase(sub, hs[sub])


def _merge_ffn(x2, gated, os, ls, gates, wl, wa, wo, gpost, gpre, wg, wu, wd, gffn, S, tm):
    T = x2.shape[0]
    nt = S // tm
    row = lambda w: pl.BlockSpec((tm, w), lambda i: (i, 0))
    vec = _const_spec((1, D_MODEL))
    return pl.pallas_call(
        functools.partial(_merge_ffn_kernel, tm=tm),
        grid=(T // tm,),
        in_specs=[row(D_MODEL), row(D_RNN)]
                 + [_grouped_spec(d, tm, GROUP_WIDTH, nt) for d in DILATIONS]
                 + [_grouped_spec(d, tm, HEAD_DIM, nt) for d in DILATIONS]
                 + [row(2 * D_MODEL),
                    _const_spec((D_RNN, D_MODEL)), _const_spec((GROUP_WIDTH, D_MODEL)),
                    _const_spec((D_MODEL, D_MODEL)), vec, vec,
                    _const_spec((D_MODEL, D_FF)), _const_spec((D_MODEL, D_FF)),
                    _const_spec((D_FF, D_MODEL)), vec],
        out_specs=row(D_MODEL),
        out_shape=jax.ShapeDtypeStruct((T, D_MODEL), F32),
        scratch_shapes=[pltpu.VMEM((tm // SUB_ROWS, SUB_ROWS, D_FF), BF16),
                        pltpu.VMEM((tm // SUB_ROWS, N_GROUPS, HEADS_PER_GROUP, SUB_ROWS, HEAD_DIM), F32),
                        pltpu.VMEM((tm // SUB_ROWS, N_GROUPS, SUB_ROWS, HEAD_DIM), F32),
                        pltpu.VMEM((tm // SUB_ROWS, SUB_ROWS, D_MODEL), F32)],
        compiler_params=pltpu.CompilerParams(
            dimension_semantics=("arbitrary",), vmem_limit_bytes=VMEM_LIMIT_BYTES),
        name="merge_ffn",
    )(x2, gated, *os, *ls, gates, wl, wa, wo, gpost, gpre, wg, wu, wd, gffn)


def _pack_lru_gate(w):
    w4 = w.reshape(N_LRU_TILES, LRU_PACK, LRU_BLOCK, LRU_BLOCK)
    eye = jnp.eye(LRU_PACK, dtype=w.dtype)
    packed = jnp.einsum('jacd,ab->jacbd', w4, eye)
    return packed.reshape(N_LRU_TILES, V7X_MXU_DIM, V7X_MXU_DIM).astype(BF16)


def kernel(x, positions, pre_mix_norm, w_in, conv_w, conv_b, w_rg, b_rg, w_ig, b_ig, lru_lambda,
           w_lru_proj, w_attn_proj, w_out, post_mix_norm, pre_ffn_norm, w_ffn_gate, w_ffn_up,
           w_ffn_down, post_ffn_norm):
    B, S, D = x.shape
    assert D == D_MODEL and S % TM == 0 and TM % SUB_ROWS == 0
    assert SUB_ROWS % (DILATIONS[-1] * 2 * V7X_SUBLANES) == 0
    assert pre_mix_norm.shape[0] == 1, "single-layer block"
    T = B * S

    inv_freq = ROPE_THETA ** (-jnp.arange(0, ROPE_DIM, 2, dtype=F32) / ROPE_DIM)
    freq = inv_freq[:, None]

    x2 = x.reshape(T, D)
    pos2 = positions.reshape(T // TM, 1, TM)
    row = lambda p: p[0][None, :]

    gated, q1, q2, q3, k1, k2, k3, v1, v2, v3, gates = _in_proj_lru(
        x2, pos2, row(pre_mix_norm), freq, w_in[0].astype(BF16), conv_w[0], row(conv_b),
        _pack_lru_gate(w_rg[0]), row(b_rg), _pack_lru_gate(w_ig[0]), row(b_ig), row(lru_lambda),
        B, S, TM)

    os, ls = [], []
    for q, k, v in ((q1, k1, v1), (q2, k2, v2), (q3, k3, v3)):
        o, l = _attention_group(q, k, v)
        os.append(o)
        ls.append(l)

    out = _merge_ffn(x2, gated, os, ls, gates,
                     w_lru_proj[0].astype(BF16), w_attn_proj[0].astype(BF16), w_out[0].astype(BF16),
                     row(post_mix_norm), row(pre_ffn_norm), w_ffn_gate[0].astype(BF16),
                     w_ffn_up[0].astype(BF16), w_ffn_down[0].astype(BF16), row(post_ffn_norm), S, TM)
    return out.reshape(B, S, D)
```

```python
import functools
import math

import jax
import jax.numpy as jnp
from jax import lax
from jax.experimental import pallas as pl
from jax.experimental.pallas import tpu as pltpu

D_MODEL = 1024
D_RNN = 1024
N_LRU_BLOCKS = 16
LRU_BLOCK = D_RNN // N_LRU_BLOCKS
CONV_WIDTH = 4
LRU_C = 8.0
ATTN_GROUPS = ((128, 1), (512, 4), (2048, 16))
DILATIONS = tuple(d for _, d in ATTN_GROUPS)
N_GROUPS = len(ATTN_GROUPS)
HEADS_PER_GROUP = 4
HEAD_DIM = 128
GROUP_WIDTH = HEADS_PER_GROUP * HEAD_DIM
ATTN_WIDTH = N_GROUPS * GROUP_WIDTH
ROPE_DIM = HEAD_DIM // 4
ROPE_THETA = 500000.0
Q_BLOCK = 128
D_FF = ((8 * D_MODEL // 3 + 255) // 256) * 256
IN_WIDTH = 2 * D_RNN + 3 * ATTN_WIDTH + 2 * D_MODEL
EPS = 1e-6
NEG = -1e30

V7X_LANES = 128
V7X_SUBLANES = 8
V7X_MXU_DIM = 256
V7X_VMEM_BYTES = 64 * 1024 * 1024
VMEM_LIMIT_BYTES = V7X_VMEM_BYTES - 8 * 1024 * 1024

BF16 = jnp.bfloat16
F32 = jnp.float32

LRU_PACK = V7X_MXU_DIM // LRU_BLOCK
N_LRU_TILES = N_LRU_BLOCKS // LRU_PACK

TM = 512
SUB_ROWS = 256
N_CHUNKS = V7X_SUBLANES
CHUNK_LEN = SUB_ROWS // N_CHUNKS
CONV_TAIL = (CONV_WIDTH - 1) * V7X_SUBLANES

assert all(w // d == Q_BLOCK for w, d in ATTN_GROUPS), "band logic assumes window == dilation * Q_BLOCK"
assert CONV_WIDTH - 1 <= CHUNK_LEN


def _rms_norm(x, g):
    return x * lax.rsqrt(jnp.mean(x * x, axis=-1, keepdims=True) + EPS) * g


def _sigmoid(x):
    return 0.5 * jnp.tanh(0.5 * x) + 0.5


def _gelu_tanh(x):
    c = math.sqrt(2.0 / math.pi)
    return x * (0.5 * jnp.tanh(x * (c + (c * 0.044715) * (x * x))) + 0.5)


def _const_spec(shape):
    nd = len(shape)
    return pl.BlockSpec(shape, lambda *_: (0,) * nd, pipeline_mode=pl.Buffered(1))


def _slabs(width):
    return [slice(c * V7X_LANES, (c + 1) * V7X_LANES) for c in range(width // V7X_LANES)]


def _residue_major(slab_ref, d, n_rows):
    if d == 1:
        return jnp.concatenate([slab_ref[c] for c in range(slab_ref.shape[0])], axis=1)
    n = n_rows // d
    return jnp.concatenate(
        [jnp.concatenate([slab_ref[c, pl.ds(r, n, stride=d), :] for c in range(slab_ref.shape[0])], axis=1)
         for r in range(d)], axis=0)


def _regroup(slab_ref, d_in, d_out, n_rows):
    assert d_out == d_in * d_in
    n_in, n_out = n_rows // d_in, n_rows // d_out
    pieces = []
    for r in range(d_out):
        start = (r % d_in) * n_in + r // d_in
        pieces.append(jnp.concatenate(
            [slab_ref[c, pl.ds(start, n_out, stride=d_in), :] for c in range(slab_ref.shape[0])], axis=1))
    return jnp.concatenate(pieces, axis=0)


IN_CHUNK = GROUP_WIDTH


def _in_proj_lru_kernel(x_ref, pos_ref, g_ref, freq_ref, w_ref,
                        cw_ref, cb_ref, wrg_ref, brg_ref, wig_ref, big_ref, lam_ref,
                        gated_ref, q1_ref, q2_ref, q3_ref, k1_ref, k2_ref, k3_ref,
                        v1_ref, v2_ref, v3_ref, gates_ref,
                        h_scr, rope_scr, hl_scr, h4_scr, lhs_scr, gate_scr, xr_halo, hstate, g_scr, *, tm, steps_per_seq):
    half = ROPE_DIM // 2
    scale = HEAD_DIM ** -0.5
    q_base = 2 * D_RNN // IN_CHUNK
    gates_base = q_base + 3 * N_GROUPS
    seq_pos = (pl.program_id(0) % steps_per_seq) * tm
    sub8 = lax.broadcasted_iota(jnp.int32, (V7X_SUBLANES, D_RNN), 0)
    slabs = _slabs(D_MODEL)

    seg_lane = lax.broadcasted_iota(jnp.int32, (SUB_ROWS, HEAD_DIM), 1)

    def rope(y, tabs, mul):
        c, s_lo, s_hi = (tabs[:, sl] if mul == 1.0 else tabs[:, sl] * mul for sl in _slabs(3 * HEAD_DIM))
        ts = [y[:, sl] for sl in _slabs(GROUP_WIDTH)]
        packed = ts[0]
        for hd in range(1, HEADS_PER_GROUP):
            packed = jnp.where(seg_lane >= hd * ROPE_DIM, pltpu.roll(ts[hd], hd * ROPE_DIM, axis=1), packed)
        up = pltpu.roll(packed, HEAD_DIM - half, axis=1)
        dn = pltpu.roll(packed, half, axis=1)
        rot = packed * c + up * s_lo + dn * s_hi
        outs = []
        for hd, t in enumerate(ts):
            back = rot if hd == 0 else pltpu.roll(rot, HEAD_DIM - hd * ROPE_DIM, axis=1)
            outs.append(jnp.where(seg_lane < ROPE_DIM, back, t if mul == 1.0 else t * mul))
        return jnp.concatenate(outs, axis=1)

    def chunk(lhs_ref, ci):
        return jnp.dot(lhs_ref[...], w_ref[:, ci * IN_CHUNK:(ci + 1) * IN_CHUNK], preferred_element_type=F32)

    def wide(lhs, first_chunk):
        return jnp.concatenate([chunk(lhs, first_chunk + j) for j in range(D_RNN // IN_CHUNK)], axis=1)

    nl = -lam_ref[...]
    softplus = jnp.maximum(nl, 0.0) + jnp.log1p(jnp.exp(-jnp.abs(nl)))
    half_coef = (-0.5 * LRU_C * math.log2(math.e)) * softplus

    @pl.when(seq_pos == 0)
    def _():
        xr_halo[...] = jnp.zeros(xr_halo.shape, F32)
        hstate[...] = jnp.zeros(hstate.shape, F32)

    n_sub = tm // SUB_ROWS
    dyn_zero = jnp.minimum(pl.program_id(0), 0)

    def emit(sub, ref, y, d):
        n = SUB_ROWS // d
        for r in range(d):
            ref[r, sub * n:(sub + 1) * n, :] = y[r * n:(r + 1) * n, :].astype(ref.dtype)

    def qkv(sub, g):
        d, lhs_g = DILATIONS[g], lhs_scr.at[sub, 1 + g]
        tabs = _residue_major(rope_scr.at[sub], d, SUB_ROWS)
        emit(sub, (q1_ref, q2_ref, q3_ref)[g], rope(chunk(lhs_g, q_base + g), tabs, scale), d)
        emit(sub, (k1_ref, k2_ref, k3_ref)[g], rope(chunk(lhs_g, q_base + N_GROUPS + g), tabs, 1.0), d)
        emit(sub, (v1_ref, v2_ref, v3_ref)[g], chunk(lhs_g, q_base + 2 * N_GROUPS + g), d)

    for sub in range(n_sub):
        rows = slice(sub * SUB_ROWS, (sub + 1) * SUB_ROWS)
        h_sub, rope_sub = h_scr.at[sub], rope_scr.at[sub]
        h = _rms_norm(x_ref[rows, :], g_ref[...])
        for c, sl in enumerate(slabs):
            h_sub[c] = h[:, sl]
        for c in range(N_CHUNKS):
            for si, sl in enumerate(slabs):
                hl_scr[si, pl.ds(c, CHUNK_LEN, stride=N_CHUNKS), :] = h[c * CHUNK_LEN:(c + 1) * CHUNK_LEN, sl]
        lhs = lhs_scr.at[sub]
        lhs[0] = jnp.concatenate([hl_scr[si] for si in range(len(slabs))], axis=1).astype(BF16)
        lhs[1] = _residue_major(h_sub, DILATIONS[0], SUB_ROWS).astype(BF16)
        h_d4 = _residue_major(h_sub, DILATIONS[1], SUB_ROWS)
        lhs[2] = h_d4.astype(BF16)
        for si, sl in enumerate(slabs):
            h4_scr[si] = h_d4[:, sl]
        lhs[3] = _regroup(h4_scr, DILATIONS[1], DILATIONS[2], SUB_ROWS).astype(BF16)

        ang = freq_ref[...] * pos_ref[:, rows].astype(F32)
        cos_t, sin_t = jnp.cos(ang), jnp.sin(ang)
        zeros_t = jnp.zeros((half, SUB_ROWS), F32)
        per_head = lambda t: jnp.concatenate([t] * HEADS_PER_GROUP, axis=0).T
        rope_sub[0] = per_head(jnp.concatenate([cos_t, cos_t], axis=0))
        rope_sub[1] = per_head(jnp.concatenate([-sin_t, zeros_t], axis=0))
        rope_sub[2] = per_head(jnp.concatenate([zeros_t, sin_t], axis=0))

        xr = wide(lhs.at[0], 0)
        gr = wide(lhs.at[0], D_RNN // IN_CHUNK)
        qkv(sub, 0)

        prev = xr_halo[...]
        wraps = []
        for j in range(CONV_WIDTH - 1):
            grp = slice(j * V7X_SUBLANES, (j + 1) * V7X_SUBLANES)
            cur = xr[SUB_ROWS - CONV_TAIL + j * V7X_SUBLANES:SUB_ROWS - CONV_TAIL + (j + 1) * V7X_SUBLANES, :]
            wraps.append(jnp.where(sub8 == 0, pltpu.roll(prev[grp, :], 1, axis=0), pltpu.roll(cur, 1, axis=0)))
        xr_halo[...] = xr[SUB_ROWS - CONV_TAIL:, :]
        xe = jnp.concatenate(wraps + [xr], axis=0)
        y = cb_ref[...]
        for s in range(CONV_WIDTH):
            lo = CONV_TAIL - s * V7X_SUBLANES
            y = y + xe[lo:lo + SUB_ROWS, :] * cw_ref[CONV_WIDTH - 1 - s:CONV_WIDTH - s, :]
        yb = y.astype(BF16)

        for j in range(2 * D_MODEL // IN_CHUNK):
            cols = slice(j * IN_CHUNK, (j + 1) * IN_CHUNK)
            gates_ref[rows, cols] = chunk(lhs.at[1], gates_base + j).astype(gates_ref.dtype)
        qkv(sub, 1)
        pre = gate_scr.at[dyn_zero]
        for j in range(N_LRU_TILES):
            sl = slice(j * V7X_MXU_DIM, (j + 1) * V7X_MXU_DIM)
            pre[0, :, sl] = jnp.dot(yb[:, sl], wrg_ref[j], preferred_element_type=F32)
            pre[1, :, sl] = jnp.dot(yb[:, sl], wig_ref[j], preferred_element_type=F32)
        for g in range(2, N_GROUPS):
            qkv(sub, g)

        t_r = jnp.tanh(0.5 * (pre[0] + brg_ref[...]))
        ig = _sigmoid(pre[1] + big_ref[...])
        a = jnp.exp2(half_coef * t_r + half_coef)
        gap = 1.0 - a * a
        mult = jnp.where(gap > 0.0, gap * lax.rsqrt(gap), 0.0)
        if sub == 0:
            row = seq_pos + lax.broadcasted_iota(jnp.int32, (SUB_ROWS, 1), 0)
            mult = jnp.where(row == 0, 1.0, mult)
        u = mult * (ig * y)

        h_loc, p_loc = [], []
        for i in range(CHUNK_LEN):
            grp = slice(i * V7X_SUBLANES, (i + 1) * V7X_SUBLANES)
            h_loc.append(u[grp, :] if i == 0 else a[grp, :] * h_loc[-1] + u[grp, :])
            p_loc.append(a[grp, :] if i == 0 else a[grp, :] * p_loc[-1])
        h_in = jnp.where(sub8 == 0, pltpu.roll(hstate[...], 1, axis=0), 0.0)
        for c in range(1, N_CHUNKS):
            ends = h_loc[-1] + p_loc[-1] * h_in
            h_in = jnp.where(sub8 == c, pltpu.roll(ends, 1, axis=0), h_in)
        hstate[...] = h_loc[-1] + p_loc[-1] * h_in

        for i in range(CHUNK_LEN):
            grp = slice(i * V7X_SUBLANES, (i + 1) * V7X_SUBLANES)
            out = (h_loc[i] + p_loc[i] * h_in) * _gelu_tanh(gr[grp, :])
            for si, sl in enumerate(slabs):
                g_scr[si, grp, :] = out[:, sl]
        for c in range(N_CHUNKS):
            tok = slice(sub * SUB_ROWS + c * CHUNK_LEN, sub * SUB_ROWS + (c + 1) * CHUNK_LEN)
            gated_ref[tok, :] = jnp.concatenate(
                [g_scr[si, pl.ds(c, CHUNK_LEN, stride=N_CHUNKS), :] for si in range(len(slabs))],
                axis=1).astype(gated_ref.dtype)


def _grouped_spec(d, rows, width, steps_per_seq):
    return pl.BlockSpec((None, d, rows // d, width),
                        lambda i: (i // steps_per_seq, 0, i % steps_per_seq, 0))


def _in_proj_lru(x2, pos2, g, freq, w_in, conv_w, conv_b, wrg, brg, wig, big, lam, B, S, tm):
    T = B * S
    nt = S // tm
    n_sub = tm // SUB_ROWS
    row = lambda w: pl.BlockSpec((tm, w), lambda i: (i, 0))
    qkv_specs = [_grouped_spec(d, tm, GROUP_WIDTH, nt) for d in DILATIONS] * 3
    qkv_shapes = [jax.ShapeDtypeStruct((B, d, S // d, GROUP_WIDTH), BF16) for d in DILATIONS] * 3
    flat = lambda w: jax.ShapeDtypeStruct((T, w), BF16)
    vec = _const_spec((1, D_RNN))
    gate_w = _const_spec((N_LRU_TILES, V7X_MXU_DIM, V7X_MXU_DIM))
    return pl.pallas_call(
        functools.partial(_in_proj_lru_kernel, tm=tm, steps_per_seq=nt),
        grid=(T // tm,),
        in_specs=[row(D_MODEL), pl.BlockSpec((None, 1, tm), lambda i: (i, 0, 0)),
                  _const_spec((1, D_MODEL)), _const_spec((ROPE_DIM // 2, 1)),
                  _const_spec((D_MODEL, IN_WIDTH)),
                  _const_spec((CONV_WIDTH, D_RNN)), vec, gate_w, vec, gate_w, vec, vec],
        out_specs=[row(D_RNN)] + qkv_specs + [row(2 * D_MODEL)],
        out_shape=[flat(D_RNN)] + qkv_shapes + [flat(2 * D_MODEL)],
        scratch_shapes=[pltpu.VMEM((n_sub, D_MODEL // V7X_LANES, SUB_ROWS, V7X_LANES), F32),
                        pltpu.VMEM((n_sub, 3, SUB_ROWS, V7X_LANES), F32),
                        pltpu.VMEM((D_MODEL // V7X_LANES, SUB_ROWS, V7X_LANES), F32),
                        pltpu.VMEM((D_MODEL // V7X_LANES, SUB_ROWS, V7X_LANES), F32),
                        pltpu.VMEM((n_sub, 1 + N_GROUPS, SUB_ROWS, D_MODEL), BF16),
                        pltpu.VMEM((1, 2, SUB_ROWS, D_RNN), F32),
                        pltpu.VMEM((CONV_TAIL, D_RNN), F32),
                        pltpu.VMEM((V7X_SUBLANES, D_RNN), F32),
                        pltpu.VMEM((D_RNN // V7X_LANES, SUB_ROWS, V7X_LANES), F32)],
        compiler_params=pltpu.CompilerParams(
            dimension_semantics=("arbitrary",), vmem_limit_bytes=VMEM_LIMIT_BYTES),
        name="in_proj_lru",
    )(x2, pos2, g, freq, w_in, conv_w, conv_b, wrg, brg, wig, big, lam)


def _attn_kernel(q_ref, k_ref, v_ref, o_ref, lse_ref, *, n_res, n_blk):
    qi = lax.broadcasted_iota(jnp.int32, (Q_BLOCK, 2 * Q_BLOCK), 0)
    kj = lax.broadcasted_iota(jnp.int32, (Q_BLOCK, 2 * Q_BLOCK), 1)
    band2 = (kj >= qi) & (kj <= qi + Q_BLOCK)
    band1 = (lax.broadcasted_iota(jnp.int32, (Q_BLOCK, Q_BLOCK), 1)
             <= lax.broadcasted_iota(jnp.int32, (Q_BLOCK, Q_BLOCK), 0))
    lane = lax.broadcasted_iota(jnp.int32, (Q_BLOCK, HEAD_DIM), 1)
    seg = HEAD_DIM // HEADS_PER_GROUP

    def rows_of(n):
        q_rows = slice(n * Q_BLOCK, (n + 1) * Q_BLOCK)
        kv_rows = slice(max(n - 1, 0) * Q_BLOCK, (n + 1) * Q_BLOCK)
        return q_rows, kv_rows

    def scores(r, n):
        q_rows, kv_rows = rows_of(n)
        band = band1 if n == 0 else band2
        out = []
        for cols in _slabs(GROUP_WIDTH):
            s = lax.dot_general(q_ref[r, q_rows, cols], k_ref[r, kv_rows, cols],
                                (((1,), (1,)), ((), ())), preferred_element_type=F32)
            out.append(jnp.where(band, s, NEG))
        return out

    def finish(r, n, s_list):
        q_rows, kv_rows = rows_of(n)
        ps, dens = [], []
        lse_tile = jnp.zeros((Q_BLOCK, HEAD_DIM), F32)
        for hd, s in enumerate(s_list):
            m = jnp.max(s, axis=-1, keepdims=True)
            p = jnp.exp(s - m)
            den = jnp.sum(p, axis=-1, keepdims=True)
            ps.append(p.astype(BF16))
            dens.append(den)
            lse_tile = jnp.where(lane // seg == hd, m + jnp.log(den), lse_tile)
        lse_ref[r, q_rows, :] = lse_tile
        for hd, cols in enumerate(_slabs(GROUP_WIDTH)):
            o = jnp.dot(ps[hd], v_ref[r, kv_rows, cols], preferred_element_type=F32) / dens[hd]
            o_ref[r, q_rows, cols] = o.astype(o_ref.dtype)

    blocks = [(r, n) for r in range(n_res) for n in range(n_blk)]
    pending = None
    for blk in blocks:
        s_list = scores(*blk)
        if pending is not None:
            finish(*pending)
        pending = (*blk, s_list)
    finish(*pending)


def _attention_group(q, k, v):
    B, d, L, _ = q.shape
    qkv_spec = pl.BlockSpec((None, d, L, GROUP_WIDTH), lambda b: (b, 0, 0, 0))
    lse_spec = pl.BlockSpec((None, d, L, HEAD_DIM), lambda b: (b, 0, 0, 0))
    return pl.pallas_call(
        functools.partial(_attn_kernel, n_res=d, n_blk=L // Q_BLOCK),
        grid=(B,),
        in_specs=[qkv_spec, qkv_spec, qkv_spec],
        out_specs=[qkv_spec, lse_spec],
        out_shape=[jax.ShapeDtypeStruct((B, d, L, GROUP_WIDTH), BF16),
                   jax.ShapeDtypeStruct((B, d, L, HEAD_DIM), F32)],
        compiler_params=pltpu.CompilerParams(
            dimension_semantics=("arbitrary",), vmem_limit_bytes=VMEM_LIMIT_BYTES),
        name=f"attention_d{d}",
    )(q, k, v)


FF_CHUNK = V7X_MXU_DIM


def _merge_ffn_kernel(x_ref, gated_ref, o1_ref, o2_ref, o3_ref, l1_ref, l2_ref, l3_ref, gates_ref,
                      wl_ref, wa_ref, wo_ref, gpost_ref, gpre_ref, wg_ref, wu_ref, wd_ref, gffn_ref,
                      out_ref, act_ref, o_scr, l_scr, x1_scr, *, tm):
    seg = HEAD_DIM // HEADS_PER_GROUP

    def mix_phase(sub):
        rows = slice(sub * SUB_ROWS, (sub + 1) * SUB_ROWS)
        y_lru = jnp.dot(gated_ref[rows, :], wl_ref[...], preferred_element_type=F32)

        for g, (d, o_ref, l_ref) in enumerate(zip(DILATIONS, (o1_ref, o2_ref, o3_ref), (l1_ref, l2_ref, l3_ref))):
            n = SUB_ROWS // d
            src = slice(sub * n, (sub + 1) * n)
            for r in range(d):
                dst = pl.ds(r, n, stride=d) if d > 1 else pl.ds(0, n)
                l_scr[sub, g, dst, :] = l_ref[r, src, :]
                for c, sl in enumerate(_slabs(GROUP_WIDTH)):
                    o_scr[sub, g, c, dst, :] = o_ref[r, src, sl].astype(F32)

        ls = [l_scr[sub, g] for g in range(N_GROUPS)]
        mx = jnp.maximum(jnp.maximum(ls[0], ls[1]), ls[2])
        es = [jnp.exp(l - mx) for l in ls]
        inv = 1.0 / (es[0] + es[1] + es[2])
        heads = []
        for hd in range(HEADS_PER_GROUP):
            acc = None
            for g in range(N_GROUPS):
                w = (es[g] * inv)[:, hd * seg:hd * seg + 1]
                term = w * o_scr[sub, g, hd]
                acc = term if acc is None else acc + term
            heads.append(acc)
        o = jnp.concatenate(heads, axis=1).astype(BF16)
        y_attn = jnp.dot(o, wa_ref[...], preferred_element_type=F32)

        t_lru = jnp.tanh(0.5 * gates_ref[rows, :D_MODEL].astype(F32))
        t_attn = jnp.tanh(0.5 * gates_ref[rows, D_MODEL:].astype(F32))
        merged = (0.5 * ((t_lru + 1.0) * y_lru + (t_attn + 1.0) * y_attn)).astype(BF16)
        mix = jnp.dot(merged, wo_ref[...], preferred_element_type=F32)
        x1 = x_ref[rows, :] + _rms_norm(mix, gpost_ref[...])
        x1_scr[sub] = x1
        return _rms_norm(x1, gpre_ref[...]).astype(BF16)

    def ffn_phase(sub, h):
        rows = slice(sub * SUB_ROWS, (sub + 1) * SUB_ROWS)
        for c in range(0, D_FF, FF_CHUNK):
            w = min(FF_CHUNK, D_FF - c)
            gate = jnp.dot(h, wg_ref[:, c:c + w], preferred_element_type=F32)
            up = jnp.dot(h, wu_ref[:, c:c + w], preferred_element_type=F32)
            act_ref[sub, :, c:c + w] = (gate * _sigmoid(gate) * up).astype(BF16)
        f = jnp.dot(act_ref[sub], wd_ref[...], preferred_element_type=F32)
        out_ref[rows, :] = x1_scr[sub] + _rms_norm(f, gffn_ref[...])

    n_sub = tm // SUB_ROWS
    hs = [mix_phase(sub) for sub in range(n_sub)]
    for sub in range(n_sub):
        ffn_phase(sub, hs[sub])


def _merge_ffn(x2, gated, os, ls, gates, wl, wa, wo, gpost, gpre, wg, wu, wd, gffn, S, tm):
    T = x2.shape[0]
    nt = S // tm
    row = lambda w: pl.BlockSpec((tm, w), lambda i: (i, 0))
    vec = _const_spec((1, D_MODEL))
    return pl.pallas_call(
        functools.partial(_merge_ffn_kernel, tm=tm),
        grid=(T // tm,),
        in_specs=[row(D_MODEL), row(D_RNN)]
                 + [_grouped_spec(d, tm, GROUP_WIDTH, nt) for d in DILATIONS]
                 + [_grouped_spec(d, tm, HEAD_DIM, nt) for d in DILATIONS]
                 + [row(2 * D_MODEL),
                    _const_spec((D_RNN, D_MODEL)), _const_spec((GROUP_WIDTH, D_MODEL)),
                    _const_spec((D_MODEL, D_MODEL)), vec, vec,
                    _const_spec((D_MODEL, D_FF)), _const_spec((D_MODEL, D_FF)),
                    _const_spec((D_FF, D_MODEL)), vec],
        out_specs=row(D_MODEL),
        out_shape=jax.ShapeDtypeStruct((T, D_MODEL), F32),
        scratch_shapes=[pltpu.VMEM((tm // SUB_ROWS, SUB_ROWS, D_FF), BF16),
                        pltpu.VMEM((tm // SUB_ROWS, N_GROUPS, HEADS_PER_GROUP, SUB_ROWS, HEAD_DIM), F32),
                        pltpu.VMEM((tm // SUB_ROWS, N_GROUPS, SUB_ROWS, HEAD_DIM), F32),
                        pltpu.VMEM((tm // SUB_ROWS, SUB_ROWS, D_MODEL), F32)],
        compiler_params=pltpu.CompilerParams(
            dimension_semantics=("arbitrary",), vmem_limit_bytes=VMEM_LIMIT_BYTES),
        name="merge_ffn",
    )(x2, gated, *os, *ls, gates, wl, wa, wo, gpost, gpre, wg, wu, wd, gffn)


def _pack_lru_gate(w):
    w4 = w.reshape(N_LRU_TILES, LRU_PACK, LRU_BLOCK, LRU_BLOCK)
    eye = jnp.eye(LRU_PACK, dtype=w.dtype)
    packed = jnp.einsum('jacd,ab->jacbd', w4, eye)
    return packed.reshape(N_LRU_TILES, V7X_MXU_DIM, V7X_MXU_DIM).astype(BF16)


def kernel(x, positions, pre_mix_norm, w_in, conv_w, conv_b, w_rg, b_rg, w_ig, b_ig, lru_lambda,
           w_lru_proj, w_attn_proj, w_out, post_mix_norm, pre_ffn_norm, w_ffn_gate, w_ffn_up,
           w_ffn_down, post_ffn_norm):
    B, S, D = x.shape
    assert D == D_MODEL and S % TM == 0 and TM % SUB_ROWS == 0
    assert SUB_ROWS % (DILATIONS[-1] * 2 * V7X_SUBLANES) == 0
    assert pre_mix_norm.shape[0] == 1, "single-layer block"
    T = B * S

    inv_freq = ROPE_THETA ** (-jnp.arange(0, ROPE_DIM, 2, dtype=F32) / ROPE_DIM)
    freq = inv_freq[:, None]

    x2 = x.reshape(T, D)
    pos2 = positions.reshape(T // TM, 1, TM)
    row = lambda p: p[0][None, :]

    gated, q1, q2, q3, k1, k2, k3, v1, v2, v3, gates = _in_proj_lru(
        x2, pos2, row(pre_mix_norm), freq, w_in[0].astype(BF16), conv_w[0], row(conv_b),
        _pack_lru_gate(w_rg[0]), row(b_rg), _pack_lru_gate(w_ig[0]), row(b_ig), row(lru_lambda),
        B, S, TM)

    os, ls = [], []
    for q, k, v in ((q1, k1, v1), (q2, k2, v2), (q3, k3, v3)):
        o, l = _attention_group(q, k, v)
        os.append(o)
        ls.append(l)

    out = _merge_ffn(x2, gated, os, ls, gates,
                     w_lru_proj[0].astype(BF16), w_attn_proj[0].astype(BF16), w_out[0].astype(BF16),
                     row(post_mix_norm), row(pre_ffn_norm), w_ffn_gate[0].astype(BF16),
                     w_ffn_up[0].astype(BF16), w_ffn_down[0].astype(BF16), row(post_ffn_norm), S, TM)
    return out.reshape(B, S, D)
```

```python
import functools
import math

import jax
import jax.numpy as jnp
from jax import lax
from jax.experimental import pallas as pl
from jax.experimental.pallas import tpu as pltpu

D_MODEL = 1024
D_RNN = 1024
N_LRU_BLOCKS = 16
LRU_BLOCK = D_RNN // N_LRU_BLOCKS
CONV_WIDTH = 4
LRU_C = 8.0
ATTN_GROUPS = ((128, 1), (512, 4), (2048, 16))
DILATIONS = tuple(d for _, d in ATTN_GROUPS)
N_GROUPS = len(ATTN_GROUPS)
HEADS_PER_GROUP = 4
HEAD_DIM = 128
GROUP_WIDTH = HEADS_PER_GROUP * HEAD_DIM
ATTN_WIDTH = N_GROUPS * GROUP_WIDTH
ROPE_DIM = HEAD_DIM // 4
ROPE_THETA = 500000.0
Q_BLOCK = 128
D_FF = ((8 * D_MODEL // 3 + 255) // 256) * 256
IN_WIDTH = 2 * D_RNN + 3 * ATTN_WIDTH + 2 * D_MODEL
EPS = 1e-6
NEG = -1e30

V7X_LANES = 128
V7X_SUBLANES = 8
V7X_MXU_DIM = 256
V7X_VMEM_BYTES = 64 * 1024 * 1024
VMEM_LIMIT_BYTES = V7X_VMEM_BYTES - 8 * 1024 * 1024

BF16 = jnp.bfloat16
F32 = jnp.float32

LRU_PACK = V7X_MXU_DIM // LRU_BLOCK
N_LRU_TILES = N_LRU_BLOCKS // LRU_PACK

TM = 512
SUB_ROWS = 256
N_CHUNKS = V7X_SUBLANES
CHUNK_LEN = SUB_ROWS // N_CHUNKS
CONV_TAIL = (CONV_WIDTH - 1) * V7X_SUBLANES

assert all(w // d == Q_BLOCK for w, d in ATTN_GROUPS), "band logic assumes window == dilation * Q_BLOCK"
assert CONV_WIDTH - 1 <= CHUNK_LEN


def _rms_norm(x, g):
    return x * lax.rsqrt(jnp.mean(x * x, axis=-1, keepdims=True) + EPS) * g


def _sigmoid(x):
    return 0.5 * jnp.tanh(0.5 * x) + 0.5


def _gelu_tanh(x):
    c = math.sqrt(2.0 / math.pi)
    return x * (0.5 * jnp.tanh(x * (c + (c * 0.044715) * (x * x))) + 0.5)


def _const_spec(shape):
    nd = len(shape)
    return pl.BlockSpec(shape, lambda *_: (0,) * nd, pipeline_mode=pl.Buffered(1))


def _slabs(width):
    return [slice(c * V7X_LANES, (c + 1) * V7X_LANES) for c in range(width // V7X_LANES)]


def _residue_major(slab_ref, d, n_rows):
    if d == 1:
        return jnp.concatenate([slab_ref[c] for c in range(slab_ref.shape[0])], axis=1)
    n = n_rows // d
    return jnp.concatenate(
        [jnp.concatenate([slab_ref[c, pl.ds(r, n, stride=d), :] for c in range(slab_ref.shape[0])], axis=1)
         for r in range(d)], axis=0)


def _regroup(slab_ref, d_in, d_out, n_rows):
    assert d_out == d_in * d_in
    n_in, n_out = n_rows // d_in, n_rows // d_out
    pieces = []
    for r in range(d_out):
        start = (r % d_in) * n_in + r // d_in
        pieces.append(jnp.concatenate(
            [slab_ref[c, pl.ds(start, n_out, stride=d_in), :] for c in range(slab_ref.shape[0])], axis=1))
    return jnp.concatenate(pieces, axis=0)


IN_CHUNK = GROUP_WIDTH


def _in_proj_lru_kernel(x_ref, pos_ref, g_ref, freq_ref, w_ref,
                        cw_ref, cb_ref, wrg_ref, brg_ref, wig_ref, big_ref, lam_ref,
                        gated_ref, q1_ref, q2_ref, q3_ref, k1_ref, k2_ref, k3_ref,
                        v1_ref, v2_ref, v3_ref, gates_ref,
                        h_scr, rope_scr, hl_scr, h4_scr, lhs_scr, gate_scr, xr_halo, hstate, g_scr, *, tm, steps_per_seq):
    half = ROPE_DIM // 2
    scale = HEAD_DIM ** -0.5
    q_base = 2 * D_RNN // IN_CHUNK
    gates_base = q_base + 3 * N_GROUPS
    seq_pos = (pl.program_id(0) % steps_per_seq) * tm
    sub8 = lax.broadcasted_iota(jnp.int32, (V7X_SUBLANES, D_RNN), 0)
    slabs = _slabs(D_MODEL)

    seg_lane = lax.broadcasted_iota(jnp.int32, (SUB_ROWS, HEAD_DIM), 1)

    def rope(y, tabs, mul):
        c, s_lo, s_hi = (tabs[:, sl] if mul == 1.0 else tabs[:, sl] * mul for sl in _slabs(3 * HEAD_DIM))
        ts = [y[:, sl] for sl in _slabs(GROUP_WIDTH)]
        packed = ts[0]
        for hd in range(1, HEADS_PER_GROUP):
            packed = jnp.where(seg_lane >= hd * ROPE_DIM, pltpu.roll(ts[hd], hd * ROPE_DIM, axis=1), packed)
        up = pltpu.roll(packed, HEAD_DIM - half, axis=1)
        dn = pltpu.roll(packed, half, axis=1)
        rot = packed * c + up * s_lo + dn * s_hi
        outs = []
        for hd, t in enumerate(ts):
            back = rot if hd == 0 else pltpu.roll(rot, HEAD_DIM - hd * ROPE_DIM, axis=1)
            outs.append(jnp.where(seg_lane < ROPE_DIM, back, t if mul == 1.0 else t * mul))
        return jnp.concatenate(outs, axis=1)

    def chunk(lhs_ref, ci):
        return jnp.dot(lhs_ref[...], w_ref[:, ci * IN_CHUNK:(ci + 1) * IN_CHUNK], preferred_element_type=F32)

    def wide(lhs, first_chunk):
        return jnp.concatenate([chunk(lhs, first_chunk + j) for j in range(D_RNN // IN_CHUNK)], axis=1)

    nl = -lam_ref[...]
    softplus = jnp.maximum(nl, 0.0) + jnp.log1p(jnp.exp(-jnp.abs(nl)))
    half_coef = (-0.5 * LRU_C * math.log2(math.e)) * softplus

    @pl.when(seq_pos == 0)
    def _():
        xr_halo[...] = jnp.zeros(xr_halo.shape, F32)
        hstate[...] = jnp.zeros(hstate.shape, F32)

    n_sub = tm // SUB_ROWS
    dyn_zero = jnp.minimum(pl.program_id(0), 0)

    def emit(sub, ref, y, d):
        n = SUB_ROWS // d
        for r in range(d):
            ref[r, sub * n:(sub + 1) * n, :] = y[r * n:(r + 1) * n, :].astype(ref.dtype)

    def qkv(sub, g):
        d, lhs_g = DILATIONS[g], lhs_scr.at[sub, 1 + g]
        tabs = _residue_major(rope_scr.at[sub], d, SUB_ROWS)
        emit(sub, (q1_ref, q2_ref, q3_ref)[g], rope(chunk(lhs_g, q_base + g), tabs, scale), d)
        emit(sub, (k1_ref, k2_ref, k3_ref)[g], rope(chunk(lhs_g, q_base + N_GROUPS + g), tabs, 1.0), d)
        emit(sub, (v1_ref, v2_ref, v3_ref)[g], chunk(lhs_g, q_base + 2 * N_GROUPS + g), d)

    for sub in range(n_sub):
        rows = slice(sub * SUB_ROWS, (sub + 1) * SUB_ROWS)
        h_sub, rope_sub = h_scr.at[sub], rope_scr.at[sub]
        h = _rms_norm(x_ref[rows, :], g_ref[...])
        for c, sl in enumerate(slabs):
            h_sub[c] = h[:, sl]
        for c in range(N_CHUNKS):
            for si, sl in enumerate(slabs):
                hl_scr[si, pl.ds(c, CHUNK_LEN, stride=N_CHUNKS), :] = h[c * CHUNK_LEN:(c + 1) * CHUNK_LEN, sl]
        lhs = lhs_scr.at[sub]
        lhs[0] = jnp.concatenate([hl_scr[si] for si in range(len(slabs))], axis=1).astype(BF16)
        lhs[1] = _residue_major(h_sub, DILATIONS[0], SUB_ROWS).astype(BF16)
        h_d4 = _residue_major(h_sub, DILATIONS[1], SUB_ROWS)
        lhs[2] = h_d4.astype(BF16)
        for si, sl in enumerate(slabs):
            h4_scr[si] = h_d4[:, sl]
        lhs[3] = _regroup(h4_scr, DILATIONS[1], DILATIONS[2], SUB_ROWS).astype(BF16)

        ang = freq_ref[...] * pos_ref[:, rows].astype(F32)
        cos_t, sin_t = jnp.cos(ang), jnp.sin(ang)
        zeros_t = jnp.zeros((half, SUB_ROWS), F32)
        per_head = lambda t: jnp.concatenate([t] * HEADS_PER_GROUP, axis=0).T
        rope_sub[0] = per_head(jnp.concatenate([cos_t, cos_t], axis=0))
        rope_sub[1] = per_head(jnp.concatenate([-sin_t, zeros_t], axis=0))
        rope_sub[2] = per_head(jnp.concatenate([zeros_t, sin_t], axis=0))

        xr = wide(lhs.at[0], 0)
        gr = wide(lhs.at[0], D_RNN // IN_CHUNK)
        qkv(sub, 0)

        prev = xr_halo[...]
        wraps = []
        for j in range(CONV_WIDTH - 1):
            grp = slice(j * V7X_SUBLANES, (j + 1) * V7X_SUBLANES)
            cur = xr[SUB_ROWS - CONV_TAIL + j * V7X_SUBLANES:SUB_ROWS - CONV_TAIL + (j + 1) * V7X_SUBLANES, :]
            wraps.append(jnp.where(sub8 == 0, pltpu.roll(prev[grp, :], 1, axis=0), pltpu.roll(cur, 1, axis=0)))
        xr_halo[...] = xr[SUB_ROWS - CONV_TAIL:, :]
        xe = jnp.concatenate(wraps + [xr], axis=0)
        y = cb_ref[...]
        for s in range(CONV_WIDTH):
            lo = CONV_TAIL - s * V7X_SUBLANES
            y = y + xe[lo:lo + SUB_ROWS, :] * cw_ref[CONV_WIDTH - 1 - s:CONV_WIDTH - s, :]
        yb = y.astype(BF16)

        for j in range(2 * D_MODEL // IN_CHUNK):
            cols = slice(j * IN_CHUNK, (j + 1) * IN_CHUNK)
            gates_ref[rows, cols] = chunk(lhs.at[1], gates_base + j).astype(gates_ref.dtype)
        qkv(sub, 1)
        pre = gate_scr.at[dyn_zero]
        for j in range(N_LRU_TILES):
            sl = slice(j * V7X_MXU_DIM, (j + 1) * V7X_MXU_DIM)
            pre[0, :, sl] = jnp.dot(yb[:, sl], wrg_ref[j], preferred_element_type=F32)
            pre[1, :, sl] = jnp.dot(yb[:, sl], wig_ref[j], preferred_element_type=F32)
        for g in range(2, N_GROUPS):
            qkv(sub, g)

        for ct in range(D_RNN // V7X_MXU_DIM):
            cols = slice(ct * V7X_MXU_DIM, (ct + 1) * V7X_MXU_DIM)
            t_r = jnp.tanh(0.5 * (pre[0, :, cols] + brg_ref[:, cols]))
            ig = _sigmoid(pre[1, :, cols] + big_ref[:, cols])
            a = jnp.exp2(half_coef[:, cols] * t_r + half_coef[:, cols])
            gap = 1.0 - a * a
            mult = jnp.where(gap > 0.0, gap * lax.rsqrt(gap), 0.0)
            if sub == 0:
                row = seq_pos + lax.broadcasted_iota(jnp.int32, (SUB_ROWS, 1), 0)
                mult = jnp.where(row == 0, 1.0, mult)
            u = mult * (ig * y[:, cols])

            h_loc, p_loc = [], []
            for i in range(CHUNK_LEN):
                grp = slice(i * V7X_SUBLANES, (i + 1) * V7X_SUBLANES)
                h_loc.append(u[grp, :] if i == 0 else a[grp, :] * h_loc[-1] + u[grp, :])
                p_loc.append(a[grp, :] if i == 0 else a[grp, :] * p_loc[-1])
            lane8 = lax.broadcasted_iota(jnp.int32, (V7X_SUBLANES, V7X_MXU_DIM), 0)
            h_in = jnp.where(lane8 == 0, pltpu.roll(hstate[:, cols], 1, axis=0), 0.0)
            for c in range(1, N_CHUNKS):
                ends = h_loc[-1] + p_loc[-1] * h_in
                h_in = jnp.where(lane8 == c, pltpu.roll(ends, 1, axis=0), h_in)
            hstate[:, cols] = h_loc[-1] + p_loc[-1] * h_in

            for i in range(CHUNK_LEN):
                grp = slice(i * V7X_SUBLANES, (i + 1) * V7X_SUBLANES)
                out = (h_loc[i] + p_loc[i] * h_in) * _gelu_tanh(gr[grp, cols])
                for sj, sl in enumerate(_slabs(V7X_MXU_DIM)):
                    g_scr[ct * (V7X_MXU_DIM // V7X_LANES) + sj, grp, :] = out[:, sl]
        for c in range(N_CHUNKS):
            tok = slice(sub * SUB_ROWS + c * CHUNK_LEN, sub * SUB_ROWS + (c + 1) * CHUNK_LEN)
            gated_ref[tok, :] = jnp.concatenate(
                [g_scr[si, pl.ds(c, CHUNK_LEN, stride=N_CHUNKS), :] for si in range(len(slabs))],
                axis=1).astype(gated_ref.dtype)


def _grouped_spec(d, rows, width, steps_per_seq):
    return pl.BlockSpec((None, d, rows // d, width),
                        lambda i: (i // steps_per_seq, 0, i % steps_per_seq, 0))


def _in_proj_lru(x2, pos2, g, freq, w_in, conv_w, conv_b, wrg, brg, wig, big, lam, B, S, tm):
    T = B * S
    nt = S // tm
    n_sub = tm // SUB_ROWS
    row = lambda w: pl.BlockSpec((tm, w), lambda i: (i, 0))
    qkv_specs = [_grouped_spec(d, tm, GROUP_WIDTH, nt) for d in DILATIONS] * 3
    qkv_shapes = [jax.ShapeDtypeStruct((B, d, S // d, GROUP_WIDTH), BF16) for d in DILATIONS] * 3
    flat = lambda w: jax.ShapeDtypeStruct((T, w), BF16)
    vec = _const_spec((1, D_RNN))
    gate_w = _const_spec((N_LRU_TILES, V7X_MXU_DIM, V7X_MXU_DIM))
    return pl.pallas_call(
        functools.partial(_in_proj_lru_kernel, tm=tm, steps_per_seq=nt),
        grid=(T // tm,),
        in_specs=[row(D_MODEL), pl.BlockSpec((None, 1, tm), lambda i: (i, 0, 0)),
                  _const_spec((1, D_MODEL)), _const_spec((ROPE_DIM // 2, 1)),
                  _const_spec((D_MODEL, IN_WIDTH)),
                  _const_spec((CONV_WIDTH, D_RNN)), vec, gate_w, vec, gate_w, vec, vec],
        out_specs=[row(D_RNN)] + qkv_specs + [row(2 * D_MODEL)],
        out_shape=[flat(D_RNN)] + qkv_shapes + [flat(2 * D_MODEL)],
        scratch_shapes=[pltpu.VMEM((n_sub, D_MODEL // V7X_LANES, SUB_ROWS, V7X_LANES), F32),
                        pltpu.VMEM((n_sub, 3, SUB_ROWS, V7X_LANES), F32),
                        pltpu.VMEM((D_MODEL // V7X_LANES, SUB_ROWS, V7X_LANES), F32),
                        pltpu.VMEM((D_MODEL // V7X_LANES, SUB_ROWS, V7X_LANES), F32),
                        pltpu.VMEM((n_sub, 1 + N_GROUPS, SUB_ROWS, D_MODEL), BF16),
                        pltpu.VMEM((1, 2, SUB_ROWS, D_RNN), F32),
                        pltpu.VMEM((CONV_TAIL, D_RNN), F32),
                        pltpu.VMEM((V7X_SUBLANES, D_RNN), F32),
                        pltpu.VMEM((D_RNN // V7X_LANES, SUB_ROWS, V7X_LANES), F32)],
        compiler_params=pltpu.CompilerParams(
            dimension_semantics=("arbitrary",), vmem_limit_bytes=VMEM_LIMIT_BYTES),
        name="in_proj_lru",
    )(x2, pos2, g, freq, w_in, conv_w, conv_b, wrg, brg, wig, big, lam)


SCORE_BLOCKS_AHEAD = 2


def _attn_kernel(q_ref, k_ref, v_ref, o_ref, lse_ref, *, n_res, n_blk):
    qi = lax.broadcasted_iota(jnp.int32, (Q_BLOCK, 2 * Q_BLOCK), 0)
    kj = lax.broadcasted_iota(jnp.int32, (Q_BLOCK, 2 * Q_BLOCK), 1)
    band2 = (kj >= qi) & (kj <= qi + Q_BLOCK)
    band1 = (lax.broadcasted_iota(jnp.int32, (Q_BLOCK, Q_BLOCK), 1)
             <= lax.broadcasted_iota(jnp.int32, (Q_BLOCK, Q_BLOCK), 0))
    lane = lax.broadcasted_iota(jnp.int32, (Q_BLOCK, HEAD_DIM), 1)
    seg = HEAD_DIM // HEADS_PER_GROUP

    def rows_of(n):
        q_rows = slice(n * Q_BLOCK, (n + 1) * Q_BLOCK)
        kv_rows = slice(max(n - 1, 0) * Q_BLOCK, (n + 1) * Q_BLOCK)
        return q_rows, kv_rows

    def scores(r, n):
        q_rows, kv_rows = rows_of(n)
        band = band1 if n == 0 else band2
        out = []
        for cols in _slabs(GROUP_WIDTH):
            s = lax.dot_general(q_ref[r, q_rows, cols], k_ref[r, kv_rows, cols],
                                (((1,), (1,)), ((), ())), preferred_element_type=F32)
            out.append(jnp.where(band, s, NEG))
        return out

    def finish(r, n, s_list):
        q_rows, kv_rows = rows_of(n)
        ps, dens = [], []
        lse_tile = jnp.zeros((Q_BLOCK, HEAD_DIM), F32)
        for hd, s in enumerate(s_list):
            m = jnp.max(s, axis=-1, keepdims=True)
            p = jnp.exp(s - m)
            den = jnp.sum(p, axis=-1, keepdims=True)
            ps.append(p.astype(BF16))
            dens.append(den)
            lse_tile = jnp.where(lane // seg == hd, m + jnp.log(den), lse_tile)
        lse_ref[r, q_rows, :] = lse_tile
        for hd, cols in enumerate(_slabs(GROUP_WIDTH)):
            o = jnp.dot(ps[hd], v_ref[r, kv_rows, cols], preferred_element_type=F32) / dens[hd]
            o_ref[r, q_rows, cols] = o.astype(o_ref.dtype)

    blocks = [(r, n) for r in range(n_res) for n in range(n_blk)]
    pending = []
    for blk in blocks:
        pending.append((*blk, scores(*blk)))
        if len(pending) > SCORE_BLOCKS_AHEAD:
            finish(*pending.pop(0))
    for item in pending:
        finish(*item)


def _attention_group(q, k, v):
    B, d, L, _ = q.shape
    qkv_spec = pl.BlockSpec((None, d, L, GROUP_WIDTH), lambda b: (b, 0, 0, 0))
    lse_spec = pl.BlockSpec((None, d, L, HEAD_DIM), lambda b: (b, 0, 0, 0))
    return pl.pallas_call(
        functools.partial(_attn_kernel, n_res=d, n_blk=L // Q_BLOCK),
        grid=(B,),
        in_specs=[qkv_spec, qkv_spec, qkv_spec],
        out_specs=[qkv_spec, lse_spec],
        out_shape=[jax.ShapeDtypeStruct((B, d, L, GROUP_WIDTH), BF16),
                   jax.ShapeDtypeStruct((B, d, L, HEAD_DIM), F32)],
        compiler_params=pltpu.CompilerParams(
            dimension_semantics=("arbitrary",), vmem_limit_bytes=VMEM_LIMIT_BYTES),
        name=f"attention_d{d}",
    )(q, k, v)


FF_CHUNK = V7X_MXU_DIM


def _merge_ffn_kernel(x_ref, gated_ref, o1_ref, o2_ref, o3_ref, l1_ref, l2_ref, l3_ref, gates_ref,
                      wl_ref, wa_ref, wo_ref, gpost_ref, gpre_ref, wg_ref, wu_ref, wd_ref, gffn_ref,
                      out_ref, act_ref, o_scr, l_scr, x1_scr, *, tm):
    seg = HEAD_DIM // HEADS_PER_GROUP

    def mix_phase(sub):
        rows = slice(sub * SUB_ROWS, (sub + 1) * SUB_ROWS)
        y_lru = jnp.dot(gated_ref[rows, :], wl_ref[...], preferred_element_type=F32)

        for g, (d, o_ref, l_ref) in enumerate(zip(DILATIONS, (o1_ref, o2_ref, o3_ref), (l1_ref, l2_ref, l3_ref))):
            n = SUB_ROWS // d
            src = slice(sub * n, (sub + 1) * n)
            for r in range(d):
                dst = pl.ds(r, n, stride=d) if d > 1 else pl.ds(0, n)
                l_scr[sub, g, dst, :] = l_ref[r, src, :]
                for c, sl in enumerate(_slabs(GROUP_WIDTH)):
                    o_scr[sub, g, c, dst, :] = o_ref[r, src, sl].astype(F32)

        ls = [l_scr[sub, g] for g in range(N_GROUPS)]
        mx = jnp.maximum(jnp.maximum(ls[0], ls[1]), ls[2])
        es = [jnp.exp(l - mx) for l in ls]
        inv = 1.0 / (es[0] + es[1] + es[2])
        heads = []
        for hd in range(HEADS_PER_GROUP):
            acc = None
            for g in range(N_GROUPS):
                w = (es[g] * inv)[:, hd * seg:hd * seg + 1]
                term = w * o_scr[sub, g, hd]
                acc = term if acc is None else acc + term
            heads.append(acc)
        o = jnp.concatenate(heads, axis=1).astype(BF16)
        y_attn = jnp.dot(o, wa_ref[...], preferred_element_type=F32)

        t_lru = jnp.tanh(0.5 * gates_ref[rows, :D_MODEL].astype(F32))
        t_attn = jnp.tanh(0.5 * gates_ref[rows, D_MODEL:].astype(F32))
        merged = (0.5 * ((t_lru + 1.0) * y_lru + (t_attn + 1.0) * y_attn)).astype(BF16)
        mix = jnp.dot(merged, wo_ref[...], preferred_element_type=F32)
        x1 = x_ref[rows, :] + _rms_norm(mix, gpost_ref[...])
        x1_scr[sub] = x1
        return _rms_norm(x1, gpre_ref[...]).astype(BF16)

    def ffn_phase(sub, h):
        rows = slice(sub * SUB_ROWS, (sub + 1) * SUB_ROWS)
        for c in range(0, D_FF, FF_CHUNK):
            w = min(FF_CHUNK, D_FF - c)
            gate = jnp.dot(h, wg_ref[:, c:c + w], preferred_element_type=F32)
            up = jnp.dot(h, wu_ref[:, c:c + w], preferred_element_type=F32)
            act_ref[sub, :, c:c + w] = (gate * _sigmoid(gate) * up).astype(BF16)
        f = jnp.dot(act_ref[sub], wd_ref[...], preferred_element_type=F32)
        out_ref[rows, :] = x1_scr[sub] + _rms_norm(f, gffn_ref[...])

    n_sub = tm // SUB_ROWS
    hs = [mix_phase(sub) for sub in range(n_sub)]
    for sub in range(n_sub):
        ffn_phase(sub, hs[sub])


def _merge_ffn(x2, gated, os, ls, gates, wl, wa, wo, gpost, gpre, wg, wu, wd, gffn, S, tm):
    T = x2.shape[0]
    nt = S // tm
    row = lambda w: pl.BlockSpec((tm, w), lambda i: (i, 0))
    vec = _const_spec((1, D_MODEL))
    return pl.pallas_call(
        functools.partial(_merge_ffn_kernel, tm=tm),
        grid=(T // tm,),
        in_specs=[row(D_MODEL), row(D_RNN)]
                 + [_grouped_spec(d, tm, GROUP_WIDTH, nt) for d in DILATIONS]
                 + [_grouped_spec(d, tm, HEAD_DIM, nt) for d in DILATIONS]
                 + [row(2 * D_MODEL),
                    _const_spec((D_RNN, D_MODEL)), _const_spec((GROUP_WIDTH, D_MODEL)),
                    _const_spec((D_MODEL, D_MODEL)), vec, vec,
                    _const_spec((D_MODEL, D_FF)), _const_spec((D_MODEL, D_FF)),
                    _const_spec((D_FF, D_MODEL)), vec],
        out_specs=row(D_MODEL),
        out_shape=jax.ShapeDtypeStruct((T, D_MODEL), F32),
        scratch_shapes=[pltpu.VMEM((tm // SUB_ROWS, SUB_ROWS, D_FF), BF16),
                        pltpu.VMEM((tm // SUB_ROWS, N_GROUPS, HEADS_PER_GROUP, SUB_ROWS, HEAD_DIM), F32),
                        pltpu.VMEM((tm // SUB_ROWS, N_GROUPS, SUB_ROWS, HEAD_DIM), F32),
                        pltpu.VMEM((tm // SUB_ROWS, SUB_ROWS, D_MODEL), F32)],
        compiler_params=pltpu.CompilerParams(
            dimension_semantics=("arbitrary",), vmem_limit_bytes=VMEM_LIMIT_BYTES),
        name="merge_ffn",
    )(x2, gated, *os, *ls, gates, wl, wa, wo, gpost, gpre, wg, wu, wd, gffn)


def _pack_lru_gate(w):
    w4 = w.reshape(N_LRU_TILES, LRU_PACK, LRU_BLOCK, LRU_BLOCK)
    eye = jnp.eye(LRU_PACK, dtype=w.dtype)
    packed = jnp.einsum('jacd,ab->jacbd', w4, eye)
    return packed.reshape(N_LRU_TILES, V7X_MXU_DIM, V7X_MXU_DIM).astype(BF16)


def kernel(x, positions, pre_mix_norm, w_in, conv_w, conv_b, w_rg, b_rg, w_ig, b_ig, lru_lambda,
           w_lru_proj, w_attn_proj, w_out, post_mix_norm, pre_ffn_norm, w_ffn_gate, w_ffn_up,
           w_ffn_down, post_ffn_norm):
    B, S, D = x.shape
    assert D == D_MODEL and S % TM == 0 and TM % SUB_ROWS == 0
    assert SUB_ROWS % (DILATIONS[-1] * 2 * V7X_SUBLANES) == 0
    assert pre_mix_norm.shape[0] == 1, "single-layer block"
    T = B * S

    inv_freq = ROPE_THETA ** (-jnp.arange(0, ROPE_DIM, 2, dtype=F32) / ROPE_DIM)
    freq = inv_freq[:, None]

    x2 = x.reshape(T, D)
    pos2 = positions.reshape(T // TM, 1, TM)
    row = lambda p: p[0][None, :]

    gated, q1, q2, q3, k1, k2, k3, v1, v2, v3, gates = _in_proj_lru(
        x2, pos2, row(pre_mix_norm), freq, w_in[0].astype(BF16), conv_w[0], row(conv_b),
        _pack_lru_gate(w_rg[0]), row(b_rg), _pack_lru_gate(w_ig[0]), row(b_ig), row(lru_lambda),
        B, S, TM)

    os, ls = [], []
    for q, k, v in ((q1, k1, v1), (q2, k2, v2), (q3, k3, v3)):
        o, l = _attention_group(q, k, v)
        os.append(o)
        ls.append(l)

    out = _merge_ffn(x2, gated, os, ls, gates,
                     w_lru_proj[0].astype(BF16), w_attn_proj[0].astype(BF16), w_out[0].astype(BF16),
                     row(post_mix_norm), row(pre_ffn_norm), w_ffn_gate[0].astype(BF16),
                     w_ffn_up[0].astype(BF16), w_ffn_down[0].astype(BF16), row(post_ffn_norm), S, TM)
    return out.reshape(B, S, D)
```
